```python
import math
import jax
import jax.numpy as jnp
from jax import lax
import numpy as np

D_MODEL = 2048
BATCH = 32
SEQ = 256
DEPTH = 4
DEC_BATCH = 2
DEC_SEQ = 1024
PAST_LEN = 512

GRID_W = 64
H_A = 8
DK_A = 128
DV_A = 128
A_W = H_A * DV_A
CONV_K = 3
CHUNK = 64
H_B = 8
DH_B = 64
DV_B = 2 * DH_B
B_W = H_B * DV_B
ROPE_F = DH_B // 4
ROPE_BASE = 10000.0
Q_BLOCK = 128
C_GROUPS = 4
C_GW = 256
C_W = C_GROUPS * C_GW
N_BRANCH = 3
BR_W = 1024
IN_W = 4 * A_W + 4 * H_A + 3 * B_W + C_W
N_EXPERTS = 32
TOP_K = 4
D_FF = 2048
SWIGLU_LIMIT = 7.0
SWIGLU_ALPHA = 1.702
MOE_BLOCK = 256
LN_EPS = 1e-5
DN_ALPHA = (2 * DEPTH) ** 0.25
DN_BETA = (8 * DEPTH) ** -0.25

kernel_name = "hybrid_flow_backbone_step"


def _silu(x):
    return x * jax.nn.sigmoid(x)


def _layer_norm(x, g=None, b=None):
    xf = x.astype(jnp.float32)
    mu = jnp.mean(xf, axis=-1, keepdims=True)
    var = jnp.mean(jnp.square(xf - mu), axis=-1, keepdims=True)
    y = (xf - mu) * lax.rsqrt(var + LN_EPS)
    if g is not None:
        y = y * g.astype(jnp.float32) + b.astype(jnp.float32)
    return y.astype(x.dtype)


def _rms_norm(x, w):
    xf = x.astype(jnp.float32)
    y = xf * lax.rsqrt(jnp.mean(xf * xf, axis=-1, keepdims=True) + 1e-6)
    return (y * w.astype(jnp.float32)).astype(x.dtype)


def _l2norm(x):
    xf = x.astype(jnp.float32)
    return (xf * lax.rsqrt(jnp.sum(xf * xf, axis=-1, keepdims=True) + 1e-6)).astype(x.dtype)


def _short_conv(x, w):
    ch = x.shape[-1]
    pad = CONV_K // 2
    return lax.conv_general_dilated(x, w[:, None, :].astype(x.dtype), window_strides=(1,), padding=[(pad, pad)], dimension_numbers=('NWC', 'WIO', 'NWC'), feature_group_count=ch)


def _gated_delta_chunked(q, k, v, g, beta, s0):
    f32 = jnp.float32
    bn, t_len, h, dk = q.shape
    dv = v.shape[-1]
    n = t_len // CHUNK

    def chunks(t):
        t = t.astype(f32).reshape((bn, n, CHUNK, h) + t.shape[3:])
        return jnp.moveaxis(t, 3, 1)

    qc = chunks(q) * (dk ** -0.5)
    kc = chunks(k)
    vc = chunks(v)
    bc = chunks(beta)
    gc = jnp.cumsum(chunks(g), axis=-1)
    incl = jnp.tril(jnp.ones((CHUNK, CHUNK), dtype=bool))
    strict = jnp.tril(jnp.ones((CHUNK, CHUNK), dtype=bool), -1)
    diff = gc[..., :, None] - gc[..., None, :]
    decay = jnp.where(incl, jnp.exp(jnp.where(incl, diff, 0.0)), 0.0)
    kb = kc * bc[..., None]
    lmat = jnp.where(strict, jnp.einsum('bhncd,bhnsd->bhncs', kb, kc) * decay, 0.0)
    eye = jnp.eye(CHUNK, dtype=f32)
    tmat = lax.linalg.triangular_solve(eye + lmat, jnp.broadcast_to(eye, lmat.shape), left_side=True, lower=True, unit_diagonal=True)
    u = jnp.einsum('bhncs,bhnsd->bhncd', tmat, vc * bc[..., None])
    w = jnp.einsum('bhncs,bhnsd->bhncd', tmat, kb * jnp.exp(gc)[..., None])
    attn = jnp.where(incl, jnp.einsum('bhncd,bhnsd->bhncs', qc, kc) * decay, 0.0)

    def step(s, xs):
        q_i, k_i, u_i, w_i, g_i, a_i = xs
        v_new = u_i - jnp.einsum('bhck,bhkv->bhcv', w_i, s)
        o_i = jnp.einsum('bhck,bhkv->bhcv', q_i * jnp.exp(g_i)[..., None], s) + jnp.einsum('bhcs,bhsv->bhcv', a_i, v_new)
        g_last = g_i[..., -1]
        k_dec = k_i * jnp.exp(g_last[..., None] - g_i)[..., None]
        s = s * jnp.exp(g_last)[..., None, None] + jnp.einsum('bhck,bhcv->bhkv', k_dec, v_new)
        return s, o_i

    xs = tuple(jnp.moveaxis(t, 2, 0) for t in (qc, kc, u, w, gc, attn))
    s_fin, o = lax.scan(step, s0.astype(f32), xs)
    o = jnp.moveaxis(jnp.moveaxis(o, 0, 2), 1, 3).reshape(bn, t_len, h, dv)
    return o.astype(v.dtype), s_fin


def _rope_2d(x, cos, sin):
    xs = x.reshape(x.shape[:-1] + (2, 2, ROPE_F))
    a, b = xs[..., 0, :], xs[..., 1, :]
    cs, sn = cos[:, None, None], sin[:, None, None]
    out = jnp.stack([a * cs - b * sn, b * cs + a * sn], axis=-2)
    return out.reshape(x.shape).astype(x.dtype)


def _diff_attention(q, k, v, lam):
    bn, tq, h, _, dh = q.shape
    nb = tq // Q_BLOCK
    qb = jnp.moveaxis(q.reshape(bn, nb, Q_BLOCK, h, 2, dh), 1, 0)
    scale = dh ** -0.5

    def one_block(qi):
        s = jnp.einsum('bqhmd,bkhmd->bhmqk', qi, k, preferred_element_type=jnp.float32) * scale
        pr = jax.nn.softmax(s, axis=-1)
        a = pr[:, :, 0] - lam * pr[:, :, 1]
        return jnp.einsum('bhqk,bkhd->bqhd', a.astype(v.dtype), v)

    o = lax.map(one_block, qb)
    return jnp.moveaxis(o, 0, 1).reshape(bn, tq, h, v.shape[-1])


def _clamped_swiglu(hb):
    gate, up = hb[..., ::2], hb[..., 1::2]
    gate = jnp.minimum(gate, SWIGLU_LIMIT)
    up = jnp.clip(up, -SWIGLU_LIMIT, SWIGLU_LIMIT)
    return (up + 1.0) * gate * jax.nn.sigmoid(SWIGLU_ALPHA * gate)


def _moe(x, p, l):
    t_len, d = x.shape
    n_assign = t_len * TOP_K
    logits = (jnp.einsum('td,de->te', x, p['router_w'][l]) + p['router_b'][l]).astype(jnp.float32)
    top_v, top_i = lax.top_k(logits, TOP_K)
    gate = jax.nn.softmax(top_v, axis=-1).reshape(-1)
    e_flat = top_i.reshape(-1)
    order = jnp.argsort(e_flat)
    e_sorted = e_flat[order]
    tok_sorted = order // TOP_K
    counts = jnp.bincount(e_flat, length=N_EXPERTS)
    padded = (counts + MOE_BLOCK - 1) // MOE_BLOCK * MOE_BLOCK
    pad_end = jnp.cumsum(padded)
    pad_start = pad_end - padded
    start = jnp.cumsum(counts) - counts
    dest = pad_start[e_sorted] + (jnp.arange(n_assign) - start[e_sorted])
    n_blocks = -(-(n_assign + N_EXPERTS * (MOE_BLOCK - 1)) // MOE_BLOCK)
    x_disp = jnp.zeros((n_blocks * MOE_BLOCK, d), x.dtype).at[dest].set(x[tok_sorted])
    block_e = jnp.minimum(jnp.searchsorted(pad_end, jnp.arange(n_blocks) * MOE_BLOCK, side='right'), N_EXPERTS - 1)

    def run_block(args):
        xb, e = args
        hb = jnp.einsum('td,df->tf', xb, p['w_gu'][l, e]) + p['b_gu'][l, e]
        return jnp.einsum('tf,fd->td', _clamped_swiglu(hb), p['w_down'][l, e]) + p['b_down'][l, e]

    y_disp = lax.map(run_block, (x_disp.reshape(n_blocks, MOE_BLOCK, d), block_e)).reshape(n_blocks * MOE_BLOCK, d)
    y = y_disp[dest] * gate[order][:, None].astype(y_disp.dtype)
    return jax.ops.segment_sum(y, tok_sorted, num_segments=t_len).astype(x.dtype)


def _token_mixer(h, p, l, rope, ctx_k, ctx_v, s0):
    bn, t_len, _ = h.shape
    proj = jnp.einsum('btd,de->bte', h, p['w_in'][l])
    sizes = (3 * A_W, A_W, 2 * H_A, 2 * H_A, H_B * 2 * DH_B, H_B * 2 * DH_B, B_W, C_W)
    idx = [int(i) for i in np.cumsum(sizes)[:-1]]
    qkv_a, z_a, a_a, b_a, q_b, k_b, v_b, x_c = jnp.split(proj, idx, axis=-1)

    qkv_a = _silu(_short_conv(qkv_a, p['conv_w'][l]))
    q_a, k_a, v_a = jnp.split(qkv_a, 3, axis=-1)
    q_a = _l2norm(q_a.reshape(bn, t_len, H_A, DK_A))
    k_a = _l2norm(k_a.reshape(bn, t_len, H_A, DK_A))
    v_a = v_a.reshape(bn, t_len, H_A, DV_A)
    a_a = a_a.reshape(bn, t_len, 2, H_A).astype(jnp.float32)
    g = -jnp.exp(p['a_log'][l].astype(jnp.float32)) * jax.nn.softplus(a_a + p['dt_bias'][l].astype(jnp.float32))
    beta = jax.nn.sigmoid(b_a.reshape(bn, t_len, 2, H_A).astype(jnp.float32))
    o_f, s_f = _gated_delta_chunked(q_a, k_a, v_a, g[:, :, 0], beta[:, :, 0], s0[:, 0])
    rev = lambda t: jnp.flip(t, axis=1)
    o_r, s_r = _gated_delta_chunked(rev(q_a), rev(k_a), rev(v_a), rev(g[:, :, 1]), rev(beta[:, :, 1]), s0[:, 1])
    o_a = _rms_norm(o_f + rev(o_r), p['gdn_norm_w'][l]) * _silu(z_a.reshape(bn, t_len, H_A, DV_A))
    s_new = jnp.stack([s_f, s_r], axis=1)

    q_b = q_b.reshape(bn, t_len, H_B, 2, DH_B)
    k_b = k_b.reshape(bn, t_len, H_B, 2, DH_B)
    v_b = v_b.reshape(bn, t_len, H_B, DV_B)
    if rope is None:
        q_r, keys, vals = q_b, k_b, v_b
    else:
        cos, sin = rope
        q_r = _rope_2d(q_b, cos, sin)
        keys = jnp.concatenate([_rope_2d(k_b, cos, sin), ctx_k.astype(k_b.dtype)], axis=1)
        vals = jnp.concatenate([v_b, ctx_v.astype(v_b.dtype)], axis=1)
    lp = p['lam_params'][l].astype(jnp.float32)
    lam_init = 0.8 - 0.6 * math.exp(-0.3 * l)
    lam = jnp.exp(jnp.sum(lp[0] * lp[1])) - jnp.exp(jnp.sum(lp[2] * lp[3])) + lam_init
    o_b = _diff_attention(q_r, keys, vals, lam)
    o_b = (_rms_norm(o_b, p['subln_w'][l]) * (1.0 - lam_init)).astype(h.dtype)

    xc = x_c.reshape(bn, t_len, C_GROUPS, C_GW).astype(jnp.float32)
    o_c = jnp.real(jnp.fft.fft2(xc, axes=(1, 3), norm='ortho')).astype(h.dtype)

    branches = jnp.stack([o_a.reshape(bn, t_len, A_W).astype(h.dtype), o_b.reshape(bn, t_len, B_W), o_c.reshape(bn, t_len, C_W)], axis=2)
    proj_br = jnp.einsum('btnc,ncd->btnd', branches, p['w_br'][l])
    gates = jax.nn.sigmoid(jnp.einsum('btd,dne->btne', h, p['w_gate'][l]) + p['b_gate'][l])
    out = jnp.einsum('btd,de->bte', jnp.sum(gates * proj_br, axis=2), p['w_out'][l])
    return out, k_b, v_b, s_new


def _trunk_layer(x, cond, p, l, rope, ctx_k, ctx_v, s0):
    mod = (jnp.einsum('bd,de->be', cond, p['w_ada'][l]) + p['b_ada'][l])[:, None, :]
    sh1, sc1, g1, sh2, sc2, g2 = jnp.split(mod, 6, axis=-1)
    h = _layer_norm(x) * (1.0 + sc1) + sh1
    mix, k_new, v_new, s_new = _token_mixer(h, p, l, rope, ctx_k, ctx_v, s0)
    x = _layer_norm(DN_ALPHA * x + g1 * mix, p['ln1_g'][l], p['ln1_b'][l])
    h = _layer_norm(x) * (1.0 + sc2) + sh2
    y = _moe(h.reshape(-1, D_MODEL), p, l).reshape(x.shape)
    x = _layer_norm(DN_ALPHA * x + g2 * y, p['ln2_g'][l], p['ln2_b'][l])
    return x, k_new, v_new, s_new


def setup_inputs(seed: int = 0) -> dict:
    key = jax.random.key(seed)
    ks = jax.random.split(key, 32)
    f32 = jnp.float32
    d = D_MODEL

    def nrm(k, shape, s):
        return jax.random.normal(k, shape, f32) * s

    dt = jnp.exp(jax.random.uniform(ks[12], (DEPTH, 2, H_A), f32, math.log(1e-3), math.log(1e-1)))
    return {
        "x_prompt": nrm(ks[0], (BATCH, SEQ, d), 1.0),
        "x_sample": nrm(ks[1], (DEC_BATCH, DEC_SEQ, d), 1.0),
        "cache_k": nrm(ks[2], (DEC_BATCH, DEPTH, PAST_LEN, H_B, 2, DH_B), 1.0),
        "cache_v": nrm(ks[3], (DEC_BATCH, DEPTH, PAST_LEN, H_B, DV_B), 1.0),
        "state_delta": nrm(ks[4], (DEC_BATCH, DEPTH, 2, H_A, DK_A, DV_A), 0.1),
        "c": nrm(ks[5], (DEC_BATCH, d), 1.0),
        "c_ctx": nrm(ks[6], (d,), 1.0),
        "w_ada": nrm(ks[7], (DEPTH, d, 6 * d), 0.5 * d ** -0.5),
        "b_ada": nrm(ks[8], (DEPTH, 6 * d), 0.02),
        "w_in": nrm(ks[9], (DEPTH, d, IN_W), d ** -0.5),
        "conv_w": nrm(ks[10], (DEPTH, CONV_K, 3 * A_W), CONV_K ** -0.5),
        "a_log": jnp.log(jax.random.uniform(ks[11], (DEPTH, 2, H_A), f32, 1.0, 16.0)),
        "dt_bias": dt + jnp.log(-jnp.expm1(-dt)),
        "gdn_norm_w": 1.0 + nrm(ks[13], (DEPTH, DV_A), 0.02),
        "lam_params": nrm(ks[14], (DEPTH, 4, DH_B), 0.1),
        "subln_w": 1.0 + nrm(ks[15], (DEPTH, DV_B), 0.02),
        "w_br": nrm(ks[16], (DEPTH, N_BRANCH, BR_W, d), DN_BETA * BR_W ** -0.5),
        "w_gate": nrm(ks[17], (DEPTH, d, N_BRANCH, d), d ** -0.5),
        "b_gate": nrm(ks[18], (DEPTH, N_BRANCH, d), 0.02),
        "w_out": nrm(ks[19], (DEPTH, d, d), DN_BETA * d ** -0.5),
        "ln1_g": 1.0 + nrm(ks[20], (DEPTH, d), 0.02),
        "ln1_b": nrm(ks[21], (DEPTH, d), 0.02),
        "ln2_g": 1.0 + nrm(ks[22], (DEPTH, d), 0.02),
        "ln2_b": nrm(ks[23], (DEPTH, d), 0.02),
        "router_w": nrm(ks[24], (DEPTH, d, N_EXPERTS), d ** -0.5),
        "router_b": nrm(ks[25], (DEPTH, N_EXPERTS), 0.01),
        "w_gu": nrm(ks[26], (DEPTH, N_EXPERTS, d, 2 * D_FF), d ** -0.5),
        "b_gu": nrm(ks[27], (DEPTH, N_EXPERTS, 2 * D_FF), 0.01),
        "w_down": nrm(ks[28], (DEPTH, N_EXPERTS, D_FF, d), DN_BETA * D_FF ** -0.5),
        "b_down": nrm(ks[29], (DEPTH, N_EXPERTS, d), 0.01),
    }


def reference(x_prompt, x_sample, cache_k, cache_v, state_delta, c, c_ctx, w_ada, b_ada, w_in, conv_w, a_log, dt_bias, gdn_norm_w, lam_params, subln_w, w_br, w_gate, b_gate, w_out, ln1_g, ln1_b, ln2_g, ln2_b, router_w, router_b, w_gu, b_gu, w_down, b_down):
    p = dict(w_ada=w_ada, b_ada=b_ada, w_in=w_in, conv_w=conv_w, a_log=a_log, dt_bias=dt_bias, gdn_norm_w=gdn_norm_w, lam_params=lam_params, subln_w=subln_w, w_br=w_br, w_gate=w_gate, b_gate=b_gate, w_out=w_out, ln1_g=ln1_g, ln1_b=ln1_b, ln2_g=ln2_g, ln2_b=ln2_b, router_w=router_w, router_b=router_b, w_gu=w_gu, b_gu=b_gu, w_down=w_down, b_down=b_down)

    ctx_cond = _silu(c_ctx)[None, :]
    s_zero = jnp.zeros((x_prompt.shape[0], 2, H_A, DK_A, DV_A), jnp.float32)
    xp = x_prompt
    ks, vs, ss = [], [], []
    for l in range(DEPTH):
        xp, k_l, v_l, s_l = _trunk_layer(xp, ctx_cond, p, l, None, None, None, s_zero)
        ks.append(k_l)
        vs.append(v_l)
        ss.append(s_l)
    new_cache_k = jnp.stack(ks, axis=1)
    new_cache_v = jnp.stack(vs, axis=1)
    new_state_delta = jnp.stack(ss, axis=1)

    n_lat = x_sample.shape[1]
    rows = n_lat // GRID_W
    row = jnp.repeat(jnp.arange(rows, dtype=jnp.float32), GRID_W)
    col = jnp.tile(jnp.arange(GRID_W, dtype=jnp.float32), rows)
    inv_freq = ROPE_BASE ** (-jnp.arange(ROPE_F, dtype=jnp.float32) / ROPE_F)
    ang = jnp.stack([row[:, None] * inv_freq, col[:, None] * inv_freq], axis=1)
    rope = (jnp.cos(ang), jnp.sin(ang))
    lat_cond = _silu(c)
    xs = x_sample
    for l in range(DEPTH):
        xs, _, _, _ = _trunk_layer(xs, lat_cond, p, l, rope, cache_k[:, l], cache_v[:, l], state_delta[:, l])

    return (xp, xs, new_cache_k, new_cache_v, new_state_delta)
```

```python
import functools
import math

import numpy as np
import jax
import jax.numpy as jnp
from jax import lax
from jax.experimental import pallas as pl
from jax.experimental.pallas import tpu as pltpu

F32 = jnp.float32
BF16 = jnp.bfloat16

D_MODEL = 2048
BATCH = 32
SEQ = 256
DEPTH = 4
DEC_BATCH = 2
DEC_SEQ = 1024
PAST_LEN = 512
GRID_W = 64
H_A = 8
DK_A = 128
DV_A = 128
A_W = H_A * DV_A
CONV_K = 3
CHUNK = 64
H_B = 8
DH_B = 64
DV_B = 2 * DH_B
B_W = H_B * DV_B
ROPE_F = DH_B // 4
ROPE_BASE = 10000.0
C_GROUPS = 4
C_GW = 256
C_W = C_GROUPS * C_GW
N_BRANCH = 3
BR_W = 1024
N_EXPERTS = 32
TOP_K = 4
D_FF = 2048
SWIGLU_LIMIT = 7.0
SWIGLU_ALPHA = 1.702
LN_EPS = 1e-5
DN_ALPHA = (2 * DEPTH) ** 0.25

N_CTX = BATCH * SEQ
N_LAT = DEC_BATCH * DEC_SEQ
N_TOK = N_CTX + N_LAT

V7X_VMEM_BYTES = 64 * 1024 * 1024
VMEM_LIMIT = V7X_VMEM_BYTES - 8 * 1024 * 1024
LANES = 128

OFF_QKV_A = 0
OFF_Z_A = 3 * A_W
OFF_Q_B = OFF_Z_A + A_W
OFF_K_B = OFF_Q_B + B_W
OFF_V_B = OFF_K_B + B_W
OFF_X_C = OFF_V_B + B_W
MAIN_W = OFF_X_C + C_W


def _cparams(n_axes):
    return pltpu.CompilerParams(dimension_semantics=("arbitrary",) * n_axes, vmem_limit_bytes=VMEM_LIMIT)


def _mm_kernel(x_ref, w_ref, b_ref, o_ref, wbf_ref, *, act):
    @pl.when(pl.program_id(1) == 0)
    def _():
        wbf_ref[...] = w_ref[...].astype(BF16)

    acc = jnp.dot(x_ref[...], wbf_ref[...], preferred_element_type=F32) + b_ref[...]
    if act == "sigmoid":
        acc = jax.nn.sigmoid(acc)
    o_ref[...] = acc.astype(o_ref.dtype)


def _mm(x, w, w_idx, bias=None, act=None, out_dtype=F32, bm=512, bn=1024):
    m, k = x.shape
    n = w.shape[-1]
    bm = min(bm, m)
    bn = min(bn, n)
    assert m % bm == 0 and n % bn == 0
    if bias is None:
        bias = jnp.zeros((1, n), F32)
    return pl.pallas_call(
        functools.partial(_mm_kernel, act=act),
        grid=(n // bn, m // bm),
        in_specs=[
            pl.BlockSpec((bm, k), lambda j, i: (i, 0)),
            pl.BlockSpec((None, k, bn), lambda j, i: (w_idx, 0, j)),
            pl.BlockSpec((1, bn), lambda j, i: (0, j)),
        ],
        out_specs=pl.BlockSpec((bm, bn), lambda j, i: (i, j)),
        out_shape=jax.ShapeDtypeStruct((m, n), out_dtype),
        scratch_shapes=[pltpu.VMEM((k, bn), BF16)],
        compiler_params=_cparams(2),
        name="dense_mm",
    )(x, w, bias)


def _split_bf16(x):
    hi = x.astype(BF16)
    lo = (x - hi.astype(F32)).astype(BF16)
    return hi, lo


def _router_kernel(x_ref, w_ref, b_ref, o_ref):
    xh, xl = _split_bf16(x_ref[...])
    wh, wl = _split_bf16(w_ref[...])
    acc = jnp.dot(xh, wh, preferred_element_type=F32)
    acc += jnp.dot(xl, wh, preferred_element_type=F32)
    acc += jnp.dot(xh, wl, preferred_element_type=F32)
    o_ref[...] = acc + b_ref[...]


def _router_logits(x, w, b, bm=512):
    m, k = x.shape
    n = w.shape[-1]
    return pl.pallas_call(
        _router_kernel,
        grid=(m // bm,),
        in_specs=[
            pl.BlockSpec((bm, k), lambda i: (i, 0)),
            pl.BlockSpec((k, n), lambda i: (0, 0)),
            pl.BlockSpec((1, n), lambda i: (0, 0)),
        ],
        out_specs=pl.BlockSpec((bm, n), lambda i: (i, 0)),
        out_shape=jax.ShapeDtypeStruct((m, n), F32),
        compiler_params=_cparams(1),
        name="router_logits",
    )(x, w, b)


def _attn_kernel(lam_ref, q_ref, k_ref, v_ref, w_ref, o_ref, *, scale, post_scale):
    q = q_ref[0].astype(BF16)
    k = k_ref[0].astype(BF16)
    v = v_ref[0].astype(BF16)
    lam = lam_ref[0]

    def softmax_map(qm, km):
        s = lax.dot_general(qm, km, (((1,), (1,)), ((), ())), preferred_element_type=F32) * scale
        e = jnp.exp(s - jnp.max(s, axis=-1, keepdims=True))
        return e / jnp.sum(e, axis=-1, keepdims=True)

    p1 = softmax_map(q[:, :DH_B], k[:, :DH_B])
    p2 = softmax_map(q[:, DH_B:], k[:, DH_B:])
    a = (p1 - lam * p2).astype(BF16)
    o = jnp.dot(a, v, preferred_element_type=F32)
    o = o * lax.rsqrt(jnp.mean(o * o, axis=-1, keepdims=True) + 1e-6)
    o_ref[0] = o * w_ref[...] * post_scale


def _diff_attention(q, q_off, k, k_off, v, v_off, lam, subln_w, post_scale, bq=256):
    bn, tq = q.shape[:2]
    tk = k.shape[1]
    return pl.pallas_call(
        functools.partial(_attn_kernel, scale=DH_B ** -0.5, post_scale=post_scale),
        grid_spec=pltpu.PrefetchScalarGridSpec(
            num_scalar_prefetch=0,
            grid=(bn, H_B, tq // bq),
            in_specs=[
                pl.BlockSpec(memory_space=pltpu.SMEM),
                pl.BlockSpec((1, bq, LANES), lambda b, h, i: (b, i, q_off + h)),
                pl.BlockSpec((1, tk, LANES), lambda b, h, i: (b, 0, k_off + h)),
                pl.BlockSpec((1, tk, LANES), lambda b, h, i: (b, 0, v_off + h)),
                pl.BlockSpec((1, LANES), lambda b, h, i: (0, 0)),
            ],
            out_specs=pl.BlockSpec((1, bq, LANES), lambda b, h, i: (b, i, h)),
        ),
        out_shape=jax.ShapeDtypeStruct((bn, tq, B_W), F32),
        compiler_params=_cparams(3),
        name="diff_attention",
    )(lam.reshape(1), q, k, v, subln_w.reshape(1, DV_B))


def _dft_mats(n):
    j = lax.broadcasted_iota(jnp.int32, (n, n), 0)
    k = lax.broadcasted_iota(jnp.int32, (n, n), 1)
    ang = ((j * k) % n).astype(F32) * (2.0 * math.pi / n)
    s = n ** -0.5
    return (jnp.cos(ang) * s).astype(BF16), (jnp.sin(ang) * s).astype(BF16)


def _dft_kernel(x_ref, cc_ref, sc_ref, ct_ref, st_ref, o_ref):
    x = x_ref[0].astype(BF16)
    for g in range(C_GROUPS):
        xg = x[:, g * C_GW:(g + 1) * C_GW]
        a = jnp.dot(xg, cc_ref[...], preferred_element_type=F32).astype(BF16)
        b = jnp.dot(xg, sc_ref[...], preferred_element_type=F32).astype(BF16)
        o = jnp.dot(ct_ref[...], a, preferred_element_type=F32) - jnp.dot(st_ref[...], b, preferred_element_type=F32)
        o_ref[0, :, g * C_GW:(g + 1) * C_GW] = o


def _fourier_mix(x, x_off):
    bn, t = x.shape[:2]
    cc, sc = _dft_mats(C_GW)
    ct, st = _dft_mats(t)
    full = lambda shape: pl.BlockSpec(shape, lambda b: (0, 0))
    return pl.pallas_call(
        _dft_kernel,
        grid=(bn,),
        in_specs=[
            pl.BlockSpec((1, t, C_W), lambda b: (b, 0, x_off)),
            full((C_GW, C_GW)), full((C_GW, C_GW)), full((t, t)), full((t, t)),
        ],
        out_specs=pl.BlockSpec((1, t, C_W), lambda b: (b, 0, 0)),
        out_shape=jax.ShapeDtypeStruct((bn, t, C_W), F32),
        compiler_params=_cparams(1),
        name="fourier_mix",
    )(x, cc, sc, ct, st)


MOE_BM = 512
MOE_TF = 512
MOE_NF = D_FF // MOE_TF
MOE_ROWS = -(-(N_TOK * TOP_K + N_EXPERTS * (MOE_BM - 1)) // MOE_BM) * MOE_BM
MOE_NB = MOE_ROWS // MOE_BM


def _moe_kernel(be_ref, nv_ref, x_ref, wgu_ref, bgu_ref, wd_ref, bd_ref, o_ref, wdp_ref):
    i = pl.program_id(0)
    j = pl.program_id(1)
    half = MOE_TF // 2
    n_chunk = MOE_TF // LANES

    @pl.when(i < nv_ref[0])
    def _():
        hb = jnp.dot(x_ref[...], wgu_ref[...].astype(BF16), preferred_element_type=F32) + bgu_ref[...]
        even = (lax.broadcasted_iota(jnp.int32, (MOE_BM, LANES), 1) % 2) == 0
        acts = []
        for c in range(n_chunk):
            lo = hb[:, c * LANES:(c + 1) * LANES]
            hi = hb[:, (c + n_chunk) * LANES:(c + n_chunk + 1) * LANES]
            gate = jnp.where(even, lo, pltpu.roll(hi, 1, 1))
            up = jnp.where(even, pltpu.roll(lo, LANES - 1, 1), hi)
            gate = jnp.minimum(gate, SWIGLU_LIMIT)
            up = jnp.clip(up, -SWIGLU_LIMIT, SWIGLU_LIMIT)
            acts.append(((up + 1.0) * gate * jax.nn.sigmoid(SWIGLU_ALPHA * gate)).astype(BF16))
        act = jnp.concatenate(acts, axis=1)
        for cb in range(D_MODEL // LANES):
            cols = slice(cb * LANES, (cb + 1) * LANES)
            wdp_ref[cb, pl.ds(0, half, stride=2), :] = wd_ref[0:half, cols]
            wdp_ref[cb, pl.ds(1, half, stride=2), :] = wd_ref[half:MOE_TF, cols]
        wdp = jnp.concatenate([wdp_ref[cb].astype(BF16) for cb in range(D_MODEL // LANES)], axis=1)
        contrib = jnp.dot(act, wdp, preferred_element_type=F32)

        @pl.when(j == 0)
        def _():
            o_ref[...] = contrib + bd_ref[...]

        @pl.when(j > 0)
        def _():
            o_ref[...] += contrib


def _moe_ffn(x_disp, block_e, n_valid, w_gu, b_gu, w_down, b_down, l):
    def blk(i, nv):
        return jnp.minimum(i, nv[0] - 1)

    def ftile(i, j, nv):
        return jnp.where(i < nv[0], j, MOE_NF - 1)

    return pl.pallas_call(
        _moe_kernel,
        grid_spec=pltpu.PrefetchScalarGridSpec(
            num_scalar_prefetch=2,
            grid=(MOE_NB, MOE_NF),
            in_specs=[
                pl.BlockSpec((MOE_BM, D_MODEL), lambda i, j, be, nv: (blk(i, nv), 0)),
                pl.BlockSpec((None, None, D_MODEL, 2 * MOE_TF), lambda i, j, be, nv: (l, be[i], 0, ftile(i, j, nv))),
                pl.BlockSpec((None, 1, 2 * MOE_TF), lambda i, j, be, nv: (l * N_EXPERTS + be[i], 0, ftile(i, j, nv))),
                pl.BlockSpec((None, None, MOE_TF, D_MODEL), lambda i, j, be, nv: (l, be[i], ftile(i, j, nv), 0)),
                pl.BlockSpec((None, 1, D_MODEL), lambda i, j, be, nv: (l * N_EXPERTS + be[i], 0, 0)),
            ],
            out_specs=pl.BlockSpec((MOE_BM, D_MODEL), lambda i, j, be, nv: (blk(i, nv), 0)),
            scratch_shapes=[pltpu.VMEM((D_MODEL // LANES, MOE_TF, LANES), F32)],
        ),
        out_shape=jax.ShapeDtypeStruct((MOE_ROWS, D_MODEL), F32),
        compiler_params=_cparams(2),
        name="moe_ffn",
    )(block_e, n_valid, x_disp, w_gu, b_gu.reshape(DEPTH * N_EXPERTS, 1, 2 * D_FF), w_down,
      b_down.reshape(DEPTH * N_EXPERTS, 1, D_MODEL))


def _moe(h, p, l):
    n_assign = N_TOK * TOP_K
    rw = jnp.pad(p["router_w"][l], ((0, 0), (0, LANES - N_EXPERTS)))
    rb = jnp.pad(p["router_b"][l], (0, LANES - N_EXPERTS)).reshape(1, LANES)
    logits = _router_logits(h, rw, rb)[:, :N_EXPERTS]
    top_v, top_i = lax.top_k(logits, TOP_K)
    gate = jax.nn.softmax(top_v, axis=-1)
    e_flat = top_i.reshape(-1)
    order = jnp.argsort(e_flat)
    e_sorted = e_flat[order]
    tok_sorted = (order // TOP_K).astype(jnp.int32)
    counts = jnp.bincount(e_flat, length=N_EXPERTS)
    padded = (counts + MOE_BM - 1) // MOE_BM * MOE_BM
    pad_end = jnp.cumsum(padded)
    pad_start = pad_end - padded
    start = jnp.cumsum(counts) - counts
    dest_sorted = (pad_start[e_sorted] + (jnp.arange(n_assign) - start[e_sorted])).astype(jnp.int32)
    dest = jnp.zeros((n_assign,), jnp.int32).at[order].set(dest_sorted)
    src_tok = jnp.zeros((MOE_ROWS,), jnp.int32).at[dest_sorted].set(tok_sorted)
    n_valid = (pad_end[-1] // MOE_BM).astype(jnp.int32)
    blocks = jnp.arange(MOE_NB, dtype=jnp.int32)
    block_e = jnp.searchsorted(pad_end, jnp.minimum(blocks, n_valid - 1) * MOE_BM, side="right")
    block_e = jnp.minimum(block_e, N_EXPERTS - 1).astype(jnp.int32)
    x_disp = h.astype(BF16)[src_tok]
    y_disp = _moe_ffn(x_disp, block_e, n_valid.reshape(1), p["w_gu"], p["b_gu"], p["w_down"], p["b_down"], l)
    y = y_disp[dest].reshape(N_TOK, TOP_K, D_MODEL) * gate[:, :, None]
    return jnp.sum(y, axis=1)


def _silu(x):
    return x * jax.nn.sigmoid(x)


def _ln(x):
    mu = jnp.mean(x, axis=-1, keepdims=True)
    var = jnp.mean(jnp.square(x - mu), axis=-1, keepdims=True)
    return (x - mu) * lax.rsqrt(var + LN_EPS)


def _rms_norm(x, w):
    return x * lax.rsqrt(jnp.mean(x * x, axis=-1, keepdims=True) + 1e-6) * w


def _l2norm(x):
    return x * lax.rsqrt(jnp.sum(x * x, axis=-1, keepdims=True) + 1e-6)


def _short_conv(x, w):
    prev = jnp.pad(x[:, :-1], ((0, 0), (1, 0), (0, 0)))
    nxt = jnp.pad(x[:, 1:], ((0, 0), (0, 1), (0, 0)))
    return prev * w[0] + x * w[1] + nxt * w[2]


def _gated_delta_chunked(q, k, v, g, beta, s0):
    f32 = F32
    bn, t_len, h, dk = q.shape
    dv = v.shape[-1]
    n = t_len // CHUNK

    def chunks(t):
        t = t.astype(f32).reshape((bn, n, CHUNK, h) + t.shape[3:])
        return jnp.moveaxis(t, 3, 1)

    qc = chunks(q) * (dk ** -0.5)
    kc = chunks(k)
    vc = chunks(v)
    bc = chunks(beta)
    gc = jnp.cumsum(chunks(g), axis=-1)
    incl = jnp.tril(jnp.ones((CHUNK, CHUNK), dtype=bool))
    strict = jnp.tril(jnp.ones((CHUNK, CHUNK), dtype=bool), -1)
    diff = gc[..., :, None] - gc[..., None, :]
    decay = jnp.where(incl, jnp.exp(jnp.where(incl, diff, 0.0)), 0.0)
    kb = kc * bc[..., None]
    lmat = jnp.where(strict, jnp.einsum('bhncd,bhnsd->bhncs', kb, kc) * decay, 0.0)
    eye = jnp.eye(CHUNK, dtype=f32)
    tmat = lax.linalg.triangular_solve(eye + lmat, jnp.broadcast_to(eye, lmat.shape), left_side=True, lower=True, unit_diagonal=True)
    u = jnp.einsum('bhncs,bhnsd->bhncd', tmat, vc * bc[..., None])
    w = jnp.einsum('bhncs,bhnsd->bhncd', tmat, kb * jnp.exp(gc)[..., None])
    attn = jnp.where(incl, jnp.einsum('bhncd,bhnsd->bhncs', qc, kc) * decay, 0.0)

    def step(s, xs):
        q_i, k_i, u_i, w_i, g_i, a_i = xs
        v_new = u_i - jnp.einsum('bhck,bhkv->bhcv', w_i, s)
        o_i = jnp.einsum('bhck,bhkv->bhcv', q_i * jnp.exp(g_i)[..., None], s) + jnp.einsum('bhcs,bhsv->bhcv', a_i, v_new)
        g_last = g_i[..., -1]
        k_dec = k_i * jnp.exp(g_last[..., None] - g_i)[..., None]
        s = s * jnp.exp(g_last)[..., None, None] + jnp.einsum('bhck,bhcv->bhkv', k_dec, v_new)
        return s, o_i

    xs = tuple(jnp.moveaxis(t, 2, 0) for t in (qc, kc, u, w, gc, attn))
    s_fin, o = lax.scan(step, s0.astype(f32), xs)
    o = jnp.moveaxis(jnp.moveaxis(o, 0, 2), 1, 3).reshape(bn, t_len, h, dv)
    return o, s_fin


def _rope_tables():
    rows = DEC_SEQ // GRID_W
    row = jnp.repeat(jnp.arange(rows, dtype=F32), GRID_W)
    col = jnp.tile(jnp.arange(GRID_W, dtype=F32), rows)
    inv_freq = ROPE_BASE ** (-jnp.arange(ROPE_F, dtype=F32) / ROPE_F)
    ang = jnp.stack([row[:, None] * inv_freq, col[:, None] * inv_freq], axis=1)
    return jnp.cos(ang), jnp.sin(ang)


def _rope_2d(x, cos, sin):
    xs = x.reshape(x.shape[:-1] + (2, 2, ROPE_F))
    a, b = xs[..., 0, :], xs[..., 1, :]
    cs, sn = cos[:, None, None], sin[:, None, None]
    out = jnp.stack([a * cs - b * sn, b * cs + a * sn], axis=-2)
    return out.reshape(x.shape)


def _branch_a(proj, ab, p, l, s0):
    bn, t_len = proj.shape[:2]
    qkv = _silu(_short_conv(proj[..., OFF_QKV_A:OFF_Z_A], p["conv_w"][l]))
    q_a, k_a, v_a = jnp.split(qkv, 3, axis=-1)
    q_a = _l2norm(q_a.reshape(bn, t_len, H_A, DK_A))
    k_a = _l2norm(k_a.reshape(bn, t_len, H_A, DK_A))
    v_a = v_a.reshape(bn, t_len, H_A, DV_A)
    a_a = ab[..., :2 * H_A].reshape(bn, t_len, 2, H_A)
    b_a = ab[..., 2 * H_A:4 * H_A].reshape(bn, t_len, 2, H_A)
    g = -jnp.exp(p["a_log"][l]) * jax.nn.softplus(a_a + p["dt_bias"][l])
    beta = jax.nn.sigmoid(b_a)
    o_f, s_f = _gated_delta_chunked(q_a, k_a, v_a, g[:, :, 0], beta[:, :, 0], s0[:, 0])
    rev = lambda t: jnp.flip(t, axis=1)
    o_r, s_r = _gated_delta_chunked(rev(q_a), rev(k_a), rev(v_a), rev(g[:, :, 1]), rev(beta[:, :, 1]), s0[:, 1])
    z_a = proj[..., OFF_Z_A:OFF_Q_B].reshape(bn, t_len, H_A, DV_A)
    o_a = _rms_norm(o_f + rev(o_r), p["gdn_norm_w"][l]) * _silu(z_a)
    return o_a.reshape(bn, t_len, A_W), jnp.stack([s_f, s_r], axis=1)


def _lam(p, l):
    lp = p["lam_params"][l]
    lam_init = 0.8 - 0.6 * math.exp(-0.3 * l)
    lam = jnp.exp(jnp.sum(lp[0] * lp[1])) - jnp.exp(jnp.sum(lp[2] * lp[3])) + lam_init
    return lam, lam_init


def kernel(x_prompt, x_sample, cache_k, cache_v, state_delta, c, c_ctx, w_ada, b_ada, w_in, conv_w, a_log, dt_bias, gdn_norm_w, lam_params, subln_w, w_br, w_gate, b_gate, w_out, ln1_g, ln1_b, ln2_g, ln2_b, router_w, router_b, w_gu, b_gu, w_down, b_down):
    p = dict(conv_w=conv_w, a_log=a_log, dt_bias=dt_bias, gdn_norm_w=gdn_norm_w, lam_params=lam_params,
             router_w=router_w, router_b=router_b, w_gu=w_gu, b_gu=b_gu, w_down=w_down, b_down=b_down)
    d = D_MODEL
    x = jnp.concatenate([x_prompt.reshape(N_CTX, d), x_sample.reshape(N_LAT, d)], axis=0)

    cond = jnp.concatenate([_silu(c_ctx)[None, :], _silu(c)], axis=0)
    cond = jnp.pad(cond, ((0, 8 - cond.shape[0]), (0, 0))).astype(BF16)
    row_cond = jnp.concatenate([jnp.zeros((N_CTX,), jnp.int32),
                                1 + jnp.arange(N_LAT, dtype=jnp.int32) // DEC_SEQ])
    rope_cos, rope_sin = _rope_tables()
    w_gate2 = w_gate.reshape(DEPTH, d, N_BRANCH * d)
    w_br2 = w_br.reshape(DEPTH * N_BRANCH, BR_W, d)
    s_zero = jnp.zeros((BATCH, 2, H_A, DK_A, DV_A), F32)

    ks, vs, ss = [], [], []
    for l in range(DEPTH):
        mod = _mm(cond, w_ada, l, bias=b_ada[l][None, :], bm=8)[row_cond]
        sh1, sc1, g1, sh2, sc2, g2 = jnp.split(mod, 6, axis=-1)

        h = (_ln(x) * (1.0 + sc1) + sh1).astype(BF16)
        w_l = w_in[l]
        w_main = jnp.concatenate([w_l[:, :OFF_Q_B], w_l[:, OFF_Q_B + 4 * H_A:]], axis=1).astype(BF16)
        w_ab = jnp.pad(w_l[:, OFF_Q_B:OFF_Q_B + 4 * H_A], ((0, 0), (0, LANES - 4 * H_A))).astype(BF16)
        proj = _mm(h, w_main[None], 0)
        ab = _mm(h, w_ab[None], 0)[:, :4 * H_A]

        proj_c = proj[:N_CTX].reshape(BATCH, SEQ, MAIN_W)
        proj_s = proj[N_CTX:].reshape(DEC_BATCH, DEC_SEQ, MAIN_W)
        ab_c = ab[:N_CTX].reshape(BATCH, SEQ, 4 * H_A)
        ab_s = ab[N_CTX:].reshape(DEC_BATCH, DEC_SEQ, 4 * H_A)

        oa_c, s_new = _branch_a(proj_c, ab_c, p, l, s_zero)
        oa_s, _ = _branch_a(proj_s, ab_s, p, l, state_delta[:, l])

        lam, lam_init = _lam(p, l)
        qo, ko, vo = OFF_Q_B // LANES, OFF_K_B // LANES, OFF_V_B // LANES
        ob_c = _diff_attention(proj_c, qo, proj_c, ko, proj_c, vo, lam, subln_w[l], 1.0 - lam_init)
        q_s = proj_s[..., OFF_Q_B:OFF_K_B].reshape(DEC_BATCH, DEC_SEQ, H_B, 2, DH_B)
        k_s = proj_s[..., OFF_K_B:OFF_V_B].reshape(DEC_BATCH, DEC_SEQ, H_B, 2, DH_B)
        q_r = _rope_2d(q_s, rope_cos, rope_sin).reshape(DEC_BATCH, DEC_SEQ, B_W).astype(BF16)
        keys = jnp.concatenate([_rope_2d(k_s, rope_cos, rope_sin).reshape(DEC_BATCH, DEC_SEQ, B_W),
                                cache_k[:, l].reshape(DEC_BATCH, PAST_LEN, B_W)], axis=1).astype(BF16)
        vals = jnp.concatenate([proj_s[..., OFF_V_B:OFF_X_C],
                                cache_v[:, l].reshape(DEC_BATCH, PAST_LEN, B_W)], axis=1).astype(BF16)
        ob_s = _diff_attention(q_r, 0, keys, 0, vals, 0, lam, subln_w[l], 1.0 - lam_init)

        oc_c = _fourier_mix(proj_c, OFF_X_C // C_W)
        oc_s = _fourier_mix(proj_s, OFF_X_C // C_W)

        ks.append(proj_c[..., OFF_K_B:OFF_V_B].reshape(BATCH, SEQ, H_B, 2, DH_B))
        vs.append(proj_c[..., OFF_V_B:OFF_X_C].reshape(BATCH, SEQ, H_B, DV_B))
        ss.append(s_new)

        def both(a_c, a_s):
            return jnp.concatenate([a_c.reshape(N_CTX, BR_W), a_s.reshape(N_LAT, BR_W)], axis=0).astype(BF16)

        gates = _mm(h, w_gate2, l, bias=b_gate[l].reshape(1, N_BRANCH * d), act="sigmoid")
        merged = jnp.zeros((N_TOK, d), F32)
        for n, br in enumerate((both(oa_c, oa_s), both(ob_c, ob_s), both(oc_c, oc_s))):
            merged = merged + gates[:, n * d:(n + 1) * d] * _mm(br, w_br2, l * N_BRANCH + n)
        mix = _mm(merged.astype(BF16), w_out, l)

        x = _ln(DN_ALPHA * x + g1 * mix) * ln1_g[l] + ln1_b[l]
        h2 = _ln(x) * (1.0 + sc2) + sh2
        y = _moe(h2, p, l)
        x = _ln(DN_ALPHA * x + g2 * y) * ln2_g[l] + ln2_b[l]

    y_prompt = x[:N_CTX].reshape(BATCH, SEQ, d)
    y_sample = x[N_CTX:].reshape(DEC_BATCH, DEC_SEQ, d)
    return (y_prompt, y_sample, jnp.stack(ks, axis=1), jnp.stack(vs, axis=1), jnp.stack(ss, axis=1))
```

```python
import functools
import math

import jax
import jax.numpy as jnp
from jax import lax
from jax.experimental import pallas as pl
from jax.experimental.pallas import tpu as pltpu

F32 = jnp.float32
BF16 = jnp.bfloat16

D_MODEL = 2048
BATCH = 32
SEQ = 256
DEPTH = 4
DEC_BATCH = 2
DEC_SEQ = 1024
PAST_LEN = 512
GRID_W = 64
H_A = 8
DK_A = 128
DV_A = 128
A_W = H_A * DV_A
CHUNK = 64
H_B = 8
DH_B = 64
DV_B = 2 * DH_B
B_W = H_B * DV_B
ROPE_F = DH_B // 4
ROPE_BASE = 10000.0
C_GROUPS = 4
C_GW = 256
C_W = C_GROUPS * C_GW
N_BRANCH = 3
BR_W = 1024
N_EXPERTS = 32
TOP_K = 4
D_FF = 2048
SWIGLU_LIMIT = 7.0
SWIGLU_ALPHA = 1.702
LN_EPS = 1e-5
DN_ALPHA = (2 * DEPTH) ** 0.25

N_CTX = BATCH * SEQ
N_LAT = DEC_BATCH * DEC_SEQ
N_TOK = N_CTX + N_LAT
N_COND = 8

V7X_VMEM_BYTES = 64 * 1024 * 1024
VMEM_LIMIT = V7X_VMEM_BYTES - 8 * 1024 * 1024
LANES = 128

OFF_QKV_A = 0
OFF_Z_A = 3 * A_W
OFF_Q_B = OFF_Z_A + A_W
OFF_K_B = OFF_Q_B + B_W
OFF_V_B = OFF_K_B + B_W
OFF_X_C = OFF_V_B + B_W
MAIN_W = OFF_X_C + C_W


def _cparams(n_axes):
    return pltpu.CompilerParams(dimension_semantics=("arbitrary",) * n_axes, vmem_limit_bytes=VMEM_LIMIT)


def _bdot(a, b):
    return jnp.dot(a.astype(BF16), b.astype(BF16), preferred_element_type=F32)


def _bdot_nt(a, b):
    return lax.dot_general(a.astype(BF16), b.astype(BF16), (((1,), (1,)), ((), ())), preferred_element_type=F32)


def _bdot_tn(a, b):
    return lax.dot_general(a.astype(BF16), b.astype(BF16), (((0,), (0,)), ((), ())), preferred_element_type=F32)


def _cond_block(row0):
    return jnp.where(row0 < N_CTX, 0, 1 + (row0 - N_CTX) // DEC_SEQ)


NORM_TB = 256


def _ln_rows(v):
    mu = jnp.mean(v, axis=-1, keepdims=True)
    var = jnp.mean(jnp.square(v - mu), axis=-1, keepdims=True)
    return (v - mu) * lax.rsqrt(var + LN_EPS)


def _mod_spec(k):
    return pl.BlockSpec((None, 1, D_MODEL), lambda i: (_cond_block(i * NORM_TB), 0, k))


def _premix_kernel(x_ref, sh_ref, sc_ref, o_ref):
    o_ref[...] = (_ln_rows(x_ref[...]) * (1.0 + sc_ref[...]) + sh_ref[...]).astype(o_ref.dtype)


def _premix(x, mod3):
    rows = pl.BlockSpec((NORM_TB, D_MODEL), lambda i: (i, 0))
    return pl.pallas_call(
        _premix_kernel,
        grid=(N_TOK // NORM_TB,),
        in_specs=[rows, _mod_spec(0), _mod_spec(1)],
        out_specs=rows,
        out_shape=jax.ShapeDtypeStruct((N_TOK, D_MODEL), BF16),
        compiler_params=_cparams(1),
        name="premix_norm",
    )(x, mod3, mod3)


def _postmix_kernel(x_ref, mix_ref, g1_ref, lng_ref, lnb_ref, sh_ref, sc_ref, xo_ref, h_ref, hb_ref):
    xn = _ln_rows(DN_ALPHA * x_ref[...] + g1_ref[...] * mix_ref[...]) * lng_ref[...] + lnb_ref[...]
    xo_ref[...] = xn
    h = _ln_rows(xn) * (1.0 + sc_ref[...]) + sh_ref[...]
    h_ref[...] = h
    hb_ref[...] = h.astype(BF16)


def _postmix(x, mix, mod3, ln_g, ln_b):
    rows = pl.BlockSpec((NORM_TB, D_MODEL), lambda i: (i, 0))
    vec = pl.BlockSpec((1, D_MODEL), lambda i: (0, 0))
    f32 = jax.ShapeDtypeStruct((N_TOK, D_MODEL), F32)
    return pl.pallas_call(
        _postmix_kernel,
        grid=(N_TOK // NORM_TB,),
        in_specs=[rows, rows, _mod_spec(2), vec, vec, _mod_spec(3), _mod_spec(4)],
        out_specs=[rows, rows, rows],
        out_shape=[f32, f32, jax.ShapeDtypeStruct((N_TOK, D_MODEL), BF16)],
        compiler_params=_cparams(1),
        name="postmix_norm",
    )(x, mix, mod3, ln_g.reshape(1, D_MODEL), ln_b.reshape(1, D_MODEL), mod3, mod3)


def _mm_kernel(x_ref, w_ref, b_ref, o_ref, wbf_ref):
    @pl.when(pl.program_id(1) == 0)
    def _():
        wbf_ref[...] = w_ref[...].astype(BF16)

    acc = jnp.dot(x_ref[...], wbf_ref[...], preferred_element_type=F32) + b_ref[...]
    o_ref[...] = acc.astype(o_ref.dtype)


def _mm(x, w, w_idx, bias=None, out_dtype=F32, bm=512, bn=1024):
    m, k = x.shape
    n = w.shape[-1]
    bm = min(bm, m)
    bn = min(bn, n)
    assert m % bm == 0 and n % bn == 0
    if bias is None:
        bias = jnp.zeros((1, n), F32)
    return pl.pallas_call(
        _mm_kernel,
        grid=(n // bn, m // bm),
        in_specs=[
            pl.BlockSpec((bm, k), lambda j, i: (i, 0)),
            pl.BlockSpec((None, k, bn), lambda j, i: (w_idx, 0, j)),
            pl.BlockSpec((1, bn), lambda j, i: (0, j)),
        ],
        out_specs=pl.BlockSpec((bm, bn), lambda j, i: (i, j)),
        out_shape=jax.ShapeDtypeStruct((m, n), out_dtype),
        scratch_shapes=[pltpu.VMEM((k, bn), BF16)],
        compiler_params=_cparams(2),
        name="dense_mm",
    )(x, w, bias)


def _split_bf16(x):
    hi = x.astype(BF16)
    lo = (x - hi.astype(F32)).astype(BF16)
    return hi, lo


def _router_kernel(x_ref, w_ref, b_ref, o_ref):
    xh, xl = _split_bf16(x_ref[...])
    wh, wl = _split_bf16(w_ref[...])
    acc = jnp.dot(xh, wh, preferred_element_type=F32)
    acc += jnp.dot(xl, wh, preferred_element_type=F32)
    acc += jnp.dot(xh, wl, preferred_element_type=F32)
    o_ref[...] = acc + b_ref[...]


def _router_logits(x, w, b, bm=512):
    m, k = x.shape
    n = w.shape[-1]
    return pl.pallas_call(
        _router_kernel,
        grid=(m // bm,),
        in_specs=[
            pl.BlockSpec((bm, k), lambda i: (i, 0)),
            pl.BlockSpec((k, n), lambda i: (0, 0)),
            pl.BlockSpec((1, n), lambda i: (0, 0)),
        ],
        out_specs=pl.BlockSpec((bm, n), lambda i: (i, 0)),
        out_shape=jax.ShapeDtypeStruct((m, n), F32),
        compiler_params=_cparams(1),
        name="router_logits",
    )(x, w, b)


MERGE_BM = 512
MERGE_BN = 256


def _merge_kernel(h_ref, ba_ref, bb_ref, bc_ref, wg0_ref, wg1_ref, wg2_ref, bg_ref, wb0_ref, wb1_ref, wb2_ref,
                  o_ref, wgc_ref, wbc_ref):
    @pl.when(pl.program_id(1) == 0)
    def _():
        for n, (wg, wb) in enumerate(((wg0_ref, wb0_ref), (wg1_ref, wb1_ref), (wg2_ref, wb2_ref))):
            wgc_ref[n] = wg[...].astype(BF16)
            wbc_ref[n] = wb[...].astype(BF16)

    h = h_ref[...]
    acc = None
    for n, br in enumerate((ba_ref, bb_ref, bc_ref)):
        gate = jax.nn.sigmoid(jnp.dot(h, wgc_ref[n], preferred_element_type=F32) + bg_ref[n:n + 1, :])
        term = gate * jnp.dot(br[...], wbc_ref[n], preferred_element_type=F32)
        acc = term if acc is None else acc + term
    o_ref[...] = acc.astype(o_ref.dtype)


def _gated_merge(h, branches, w_gate2, b_gate, w_br2, l):
    m = h.shape[0]
    nj = D_MODEL // MERGE_BN
    row = lambda width: pl.BlockSpec((MERGE_BM, width), lambda j, i: (i, 0))
    wg = lambda n: pl.BlockSpec((None, D_MODEL, MERGE_BN), lambda j, i: (l, 0, n * nj + j))
    wb = lambda n: pl.BlockSpec((None, BR_W, MERGE_BN), lambda j, i: (l * N_BRANCH + n, 0, j))
    return pl.pallas_call(
        _merge_kernel,
        grid=(nj, m // MERGE_BM),
        in_specs=[row(D_MODEL), row(BR_W), row(BR_W), row(BR_W), wg(0), wg(1), wg(2),
                  pl.BlockSpec((None, N_BRANCH, MERGE_BN), lambda j, i: (l, 0, j)), wb(0), wb(1), wb(2)],
        out_specs=pl.BlockSpec((MERGE_BM, MERGE_BN), lambda j, i: (i, j)),
        out_shape=jax.ShapeDtypeStruct((m, D_MODEL), BF16),
        scratch_shapes=[pltpu.VMEM((N_BRANCH, D_MODEL, MERGE_BN), BF16), pltpu.VMEM((N_BRANCH, BR_W, MERGE_BN), BF16)],
        compiler_params=_cparams(2),
        name="gated_merge",
    )(h, *branches, w_gate2, w_gate2, w_gate2, b_gate, w_br2, w_br2, w_br2)


GDN_HB = 4
GDN_HG = H_A // GDN_HB
GDN_NJ = 2 * GDN_HB
GDN_GC = 4
GDN_GROUP = GDN_GC * CHUNK
GDN_BASE = 4


def _split3(x):
    hi = x.astype(BF16)
    return hi, (x - hi.astype(F32)).astype(BF16)


def _dot3(a_hi, a_lo, b_hi, b_lo):
    lhs = jnp.concatenate([a_hi, a_lo, a_hi], axis=1)
    rhs = jnp.concatenate([b_hi, b_hi, b_lo], axis=0)
    return jnp.dot(lhs, rhs, preferred_element_type=F32)


def _unit_tri_inverse(lmat, eye, base_mask, merge_masks):
    l_hi, l_lo = _split3(lmat)
    zero = jnp.zeros_like(l_hi)
    d_hi = jnp.where(base_mask, l_hi, zero)
    d_lo = jnp.where(base_mask, l_lo, zero)
    d2 = _dot3(d_hi, d_lo, d_hi, d_lo)
    yield
    imd = eye - jnp.where(base_mask, lmat, 0.0)
    tmat = imd + _dot3(*_split3(imd), *_split3(d2))
    yield
    for mask in merge_masks:
        t_hi, t_lo = _split3(tmat)
        ct = _dot3(jnp.where(mask, l_hi, zero), jnp.where(mask, l_lo, zero), t_hi, t_lo)
        yield
        tmat = tmat - _dot3(t_hi, t_lo, *_split3(ct))
        yield
    return tmat


def _gdn_kernel(gt_ref, egt_ref, q_ref, k_ref, v_ref, z_ref, cwq_ref, cwk_ref, cwv_ref, gcol_ref, bcol_ref,
                grow_ref, nw_ref, *rest, t_len, has_s0):
    if has_s0:
        s0_ref, o_ref, s_ref, qn_ref, kn_ref, vn_ref, of_ref, ob_ref = rest
    else:
        o_ref, s_ref, qn_ref, kn_ref, vn_ref, of_ref, ob_ref = rest
    n_chunks = t_len // CHUNK
    b = pl.program_id(0)
    hg = pl.program_id(1)

    row = lax.broadcasted_iota(jnp.int32, (t_len, LANES), 0)
    for h in range(GDN_HB):
        cols = slice(h * LANES, (h + 1) * LANES)
        for src, cw, dst, norm in ((q_ref, cwq_ref, qn_ref, True), (k_ref, cwk_ref, kn_ref, True),
                                   (v_ref, cwv_ref, vn_ref, False)):
            x = src[0, :, cols]
            w = cw[:, cols]
            prev = jnp.where(row == 0, 0.0, pltpu.roll(x, 1, 0))
            nxt = jnp.where(row == t_len - 1, 0.0, pltpu.roll(x, t_len - 1, 0))
            y = prev * w[0:1, :] + x * w[1:2, :] + nxt * w[2:3, :]
            y = y * jax.nn.sigmoid(y)
            if norm:
                y = y * lax.rsqrt(jnp.sum(y * y, axis=-1, keepdims=True) + 1e-6)
            dst[:, cols] = y
    if has_s0:
        s_ref[...] = s0_ref[...]
    else:
        s_ref[...] = jnp.zeros_like(s_ref)

    n_groups = t_len // GDN_GROUP
    r_i = lax.broadcasted_iota(jnp.int32, (GDN_GROUP, GDN_GROUP), 0)
    c_i = lax.broadcasted_iota(jnp.int32, (GDN_GROUP, GDN_GROUP), 1)
    same = (r_i // CHUNK) == (c_i // CHUNK)
    eye = (r_i == c_i).astype(F32)
    masks = ((same & (r_i >= c_i), same & (r_i > c_i)), (same & (r_i <= c_i), same & (r_i < c_i)))
    base_mask = (r_i // GDN_BASE) == (c_i // GDN_BASE)
    merge_masks = []
    size = GDN_BASE
    while size < CHUNK:
        merge_masks.append(((r_i // (2 * size)) == (c_i // (2 * size))) & ((r_i // size) != (c_i // size)))
        size *= 2
    scale = DK_A ** -0.5

    def recurrence(n, d, h):
        grp = n if d == 0 else n_groups - 1 - n
        r0 = grp * GDN_GROUP if isinstance(grp, int) else pl.multiple_of(grp * GDN_GROUP, GDN_GROUP)
        incl, strict = masks[d]
        jj = d * GDN_HB + h
        cols = slice(h * LANES, (h + 1) * LANES)
        gcol = gcol_ref[0, 0, grp][:, jj:jj + 1]
        beta = bcol_ref[0, 0, grp][:, jj:jj + 1]
        grow = grow_ref[0, 0, grp][jj:jj + 1, :]
        q_g = qn_ref[pl.ds(r0, GDN_GROUP), cols] * scale
        k_g = kn_ref[pl.ds(r0, GDN_GROUP), cols]
        v_g = vn_ref[pl.ds(r0, GDN_GROUP), cols]
        decay = jnp.where(incl, jnp.exp(jnp.where(incl, gcol - grow, 0.0)), 0.0)
        egc = jnp.exp(gcol)
        kb = k_g * beta
        kq = _bdot_nt(jnp.concatenate([kb, q_g], axis=0), k_g)
        yield
        lmat = jnp.where(strict, kq[:GDN_GROUP] * decay, 0.0)
        attn = jnp.where(incl, kq[GDN_GROUP:] * decay, 0.0)
        tmat = yield from _unit_tri_inverse(lmat, eye, base_mask, merge_masks)
        uw = _bdot(tmat, jnp.concatenate([v_g * beta, kb * egc], axis=1))
        yield
        u, w = uw[:, :DV_A], uw[:, DV_A:]
        qg = q_g * egc
        s_cur = s_ref[0, d, h]
        outs = [None] * GDN_GC
        for cc in (range(GDN_GC) if d == 0 else range(GDN_GC - 1, -1, -1)):
            rows = slice(cc * CHUNK, (cc + 1) * CHUNK)
            sidx = ((b * GDN_HG + hg) * n_chunks + grp * GDN_GC + cc) * GDN_NJ + jj
            ws_qs = _bdot(jnp.concatenate([w[rows], qg[rows]], axis=0), s_cur)
            yield
            v_new = u[rows] - ws_qs[:CHUNK]
            outs[cc] = ws_qs[CHUNK:] + _bdot(attn[rows, rows], v_new)
            k_dec = k_g[rows] * jnp.exp(gt_ref[sidx] - gcol[rows])
            s_cur = s_cur * egt_ref[sidx] + _bdot_tn(k_dec, v_new)
            yield
        return d, h, r0, cols, s_cur, jnp.concatenate(outs, axis=0)

    def group_step(n, carry):
        live = [recurrence(n, d, h) for d in range(2) for h in range(GDN_HB)]
        done = []
        while live:
            for gen in list(live):
                try:
                    next(gen)
                except StopIteration as stop:
                    live.remove(gen)
                    done.append(stop.value)
        for d, h, r0, cols, s_cur, o in done:
            s_ref[0, d, h] = s_cur
            (of_ref, ob_ref)[d][pl.ds(r0, GDN_GROUP), cols] = o
        return carry

    if n_groups == 1:
        group_step(0, 0)
    else:
        lax.fori_loop(0, n_groups, group_step, 0)

    for h in range(GDN_HB):
        cols = slice(h * LANES, (h + 1) * LANES)
        o = of_ref[:, cols] + ob_ref[:, cols]
        z = z_ref[0, :, cols]
        o = o * lax.rsqrt(jnp.mean(o * o, axis=-1, keepdims=True) + 1e-6) * nw_ref[...]
        o_ref[0, :, cols] = (o * (z * jax.nn.sigmoid(z))).astype(o_ref.dtype)


def _gdn_gates(ab, a_log, dt_bias):
    bn, t_len = ab.shape[:2]
    n = t_len // CHUNK
    a_a = ab[..., :2 * H_A].reshape(bn, t_len, 2, H_A)
    b_a = ab[..., 2 * H_A:].reshape(bn, t_len, 2, H_A)
    g = (-jnp.exp(a_log) * jax.nn.softplus(a_a + dt_bias)).reshape(bn, n, CHUNK, 2, H_A)
    beta = jax.nn.sigmoid(b_a).reshape(bn, n, CHUNK, 2, H_A)
    cs_f = jnp.cumsum(g[:, :, :, 0], axis=2)
    cs_b = jnp.flip(jnp.cumsum(jnp.flip(g[:, :, :, 1], axis=2), axis=2), axis=2)
    gcs = jnp.stack([cs_f, cs_b], axis=3)
    g_tot = jnp.stack([cs_f[:, :, -1], cs_b[:, :, 0]], axis=2)

    def group(t):
        t = t.reshape(bn, t_len // GDN_GROUP, GDN_GROUP, 2, GDN_HG, GDN_HB)
        return jnp.transpose(t, (0, 4, 1, 2, 3, 5)).reshape(bn, GDN_HG, t_len // GDN_GROUP, GDN_GROUP, GDN_NJ)

    gcol = group(gcs)
    grow = jnp.swapaxes(gcol, 3, 4)
    g_tot = jnp.transpose(g_tot.reshape(bn, n, 2, GDN_HG, GDN_HB), (0, 3, 1, 2, 4)).reshape(-1)
    return gcol, group(beta), grow, g_tot


def _gated_deltanet(proj, ab, conv_w, a_log, dt_bias, norm_w, s0):
    bn, t_len = proj.shape[:2]
    n = t_len // CHUNK
    gcol, bcol, grow, g_tot = _gdn_gates(ab, a_log, dt_bias)
    gw = GDN_HB * LANES
    off = lambda o: o // gw
    seq = lambda o: pl.BlockSpec((1, t_len, gw), lambda b, g, *_: (b, 0, off(o) + g))
    cw = lambda o: pl.BlockSpec((3, gw), lambda b, g, *_: (0, off(o) + g))
    ng = t_len // GDN_GROUP
    colspec = pl.BlockSpec((1, 1, ng, GDN_GROUP, GDN_NJ), lambda b, g, *_: (b, g, 0, 0, 0))
    state = pl.BlockSpec((1, 2, GDN_HB, DK_A, DV_A), lambda b, g, *_: (b, 0, g, 0, 0))
    in_specs = [seq(OFF_QKV_A), seq(OFF_QKV_A + A_W), seq(OFF_QKV_A + 2 * A_W), seq(OFF_Z_A),
                cw(0), cw(A_W), cw(2 * A_W), colspec, colspec,
                pl.BlockSpec((1, 1, ng, GDN_NJ, GDN_GROUP), lambda b, g, *_: (b, g, 0, 0, 0)),
                pl.BlockSpec((1, LANES), lambda b, g, *_: (0, 0))]
    args = [g_tot, jnp.exp(g_tot), proj, proj, proj, proj, conv_w, conv_w, conv_w, gcol, bcol, grow,
            norm_w.reshape(1, DV_A)]
    if s0 is not None:
        in_specs.append(state)
        args.append(s0)
    return pl.pallas_call(
        functools.partial(_gdn_kernel, t_len=t_len, has_s0=s0 is not None),
        grid_spec=pltpu.PrefetchScalarGridSpec(
            num_scalar_prefetch=2,
            grid=(bn, GDN_HG),
            in_specs=in_specs,
            out_specs=[pl.BlockSpec((1, t_len, gw), lambda b, g, *_: (b, 0, g)), state],
            scratch_shapes=[pltpu.VMEM((t_len, gw), F32)] * 5,
        ),
        out_shape=[jax.ShapeDtypeStruct((bn, t_len, A_W), BF16),
                   jax.ShapeDtypeStruct((bn, 2, H_A, DK_A, DV_A), F32)],
        compiler_params=_cparams(2),
        name="gated_deltanet",
    )(*args)


def _attn_kernel(lam_ref, q_ref, k_ref, v_ref, w_ref, o_ref, *, scale, post_scale):
    q = q_ref[0].astype(BF16)
    k = k_ref[0].astype(BF16)
    v = v_ref[0].astype(BF16)
    lam = lam_ref[0]

    def softmax_map(qm, km):
        s = lax.dot_general(qm, km, (((1,), (1,)), ((), ())), preferred_element_type=F32) * scale
        e = jnp.exp(s - jnp.max(s, axis=-1, keepdims=True))
        return e / jnp.sum(e, axis=-1, keepdims=True)

    p1 = softmax_map(q[:, :DH_B], k[:, :DH_B])
    p2 = softmax_map(q[:, DH_B:], k[:, DH_B:])
    a = (p1 - lam * p2).astype(BF16)
    o = jnp.dot(a, v, preferred_element_type=F32)
    o = o * lax.rsqrt(jnp.mean(o * o, axis=-1, keepdims=True) + 1e-6)
    o_ref[0] = (o * w_ref[...] * post_scale).astype(o_ref.dtype)


def _diff_attention(q, q_off, k, k_off, v, v_off, lam, subln_w, post_scale, bq=256):
    bn, tq = q.shape[:2]
    tk = k.shape[1]
    return pl.pallas_call(
        functools.partial(_attn_kernel, scale=DH_B ** -0.5, post_scale=post_scale),
        grid=(bn, H_B, tq // bq),
        in_specs=[
            pl.BlockSpec(memory_space=pltpu.SMEM),
            pl.BlockSpec((1, bq, LANES), lambda b, h, i: (b, i, q_off + h)),
            pl.BlockSpec((1, tk, LANES), lambda b, h, i: (b, 0, k_off + h)),
            pl.BlockSpec((1, tk, LANES), lambda b, h, i: (b, 0, v_off + h)),
            pl.BlockSpec((1, LANES), lambda b, h, i: (0, 0)),
        ],
        out_specs=pl.BlockSpec((1, bq, LANES), lambda b, h, i: (b, i, h)),
        out_shape=jax.ShapeDtypeStruct((bn, tq, B_W), BF16),
        compiler_params=_cparams(3),
        name="diff_attention",
    )(lam.reshape(1), q, k, v, subln_w.reshape(1, DV_B))


def _dft_mats(n):
    j = lax.broadcasted_iota(jnp.int32, (n, n), 0)
    k = lax.broadcasted_iota(jnp.int32, (n, n), 1)
    ang = ((j * k) % n).astype(F32) * (2.0 * math.pi / n)
    s = n ** -0.5
    return (jnp.cos(ang) * s).astype(BF16), (jnp.sin(ang) * s).astype(BF16)


def _dft_kernel(x_ref, cc_ref, sc_ref, ct_ref, st_ref, o_ref):
    x = x_ref[0].astype(BF16)
    for g in range(C_GROUPS):
        xg = x[:, g * C_GW:(g + 1) * C_GW]
        a = jnp.dot(xg, cc_ref[...], preferred_element_type=F32).astype(BF16)
        b = jnp.dot(xg, sc_ref[...], preferred_element_type=F32).astype(BF16)
        o = jnp.dot(ct_ref[...], a, preferred_element_type=F32) - jnp.dot(st_ref[...], b, preferred_element_type=F32)
        o_ref[0, :, g * C_GW:(g + 1) * C_GW] = o.astype(o_ref.dtype)


def _fourier_mix(x, x_off):
    bn, t = x.shape[:2]
    cc, sc = _dft_mats(C_GW)
    ct, st = _dft_mats(t)
    full = lambda shape: pl.BlockSpec(shape, lambda b: (0, 0))
    return pl.pallas_call(
        _dft_kernel,
        grid=(bn,),
        in_specs=[
            pl.BlockSpec((1, t, C_W), lambda b: (b, 0, x_off)),
            full((C_GW, C_GW)), full((C_GW, C_GW)), full((t, t)), full((t, t)),
        ],
        out_specs=pl.BlockSpec((1, t, C_W), lambda b: (b, 0, 0)),
        out_shape=jax.ShapeDtypeStruct((bn, t, C_W), BF16),
        compiler_params=_cparams(1),
        name="fourier_mix",
    )(x, cc, sc, ct, st)


MOE_BM = 512
MOE_TF = 512
MOE_NF = D_FF // MOE_TF
MOE_ROWS = -(-(N_TOK * TOP_K + N_EXPERTS * (MOE_BM - 1)) // MOE_BM) * MOE_BM
MOE_NB = MOE_ROWS // MOE_BM


def _moe_kernel(be_ref, nv_ref, x_ref, wgu_ref, bgu_ref, wd_ref, bd_ref, o_ref, wdp_ref):
    i = pl.program_id(0)
    j = pl.program_id(1)
    half = MOE_TF // 2
    n_chunk = MOE_TF // LANES

    @pl.when(i < nv_ref[0])
    def _():
        hb = jnp.dot(x_ref[...], wgu_ref[...].astype(BF16), preferred_element_type=F32) + bgu_ref[...]
        even = (lax.broadcasted_iota(jnp.int32, (MOE_BM, LANES), 1) % 2) == 0
        acts = []
        for c in range(n_chunk):
            lo = hb[:, c * LANES:(c + 1) * LANES]
            hi = hb[:, (c + n_chunk) * LANES:(c + n_chunk + 1) * LANES]
            gate = jnp.where(even, lo, pltpu.roll(hi, 1, 1))
            up = jnp.where(even, pltpu.roll(lo, LANES - 1, 1), hi)
            gate = jnp.minimum(gate, SWIGLU_LIMIT)
            up = jnp.clip(up, -SWIGLU_LIMIT, SWIGLU_LIMIT)
            acts.append(((up + 1.0) * gate * jax.nn.sigmoid(SWIGLU_ALPHA * gate)).astype(BF16))
        act = jnp.concatenate(acts, axis=1)
        for cb in range(D_MODEL // LANES):
            cols = slice(cb * LANES, (cb + 1) * LANES)
            wdp_ref[cb, pl.ds(0, half, stride=2), :] = wd_ref[0:half, cols]
            wdp_ref[cb, pl.ds(1, half, stride=2), :] = wd_ref[half:MOE_TF, cols]
        wdp = jnp.concatenate([wdp_ref[cb].astype(BF16) for cb in range(D_MODEL // LANES)], axis=1)
        contrib = jnp.dot(act, wdp, preferred_element_type=F32)

        @pl.when(j == 0)
        def _():
            o_ref[...] = contrib + bd_ref[...]

        @pl.when(j > 0)
        def _():
            o_ref[...] += contrib

    @pl.when((i >= nv_ref[0]) & (j == 0))
    def _():
        o_ref[...] = jnp.zeros_like(o_ref)


def _moe_ffn(x_disp, block_e, n_valid, w_gu, b_gu, w_down, b_down, l):
    def blk(i, nv):
        return jnp.minimum(i, nv[0] - 1)

    def ftile(i, j, nv):
        return jnp.where(i < nv[0], j, MOE_NF - 1)

    return pl.pallas_call(
        _moe_kernel,
        grid_spec=pltpu.PrefetchScalarGridSpec(
            num_scalar_prefetch=2,
            grid=(MOE_NB, MOE_NF),
            in_specs=[
                pl.BlockSpec((MOE_BM, D_MODEL), lambda i, j, be, nv: (blk(i, nv), 0)),
                pl.BlockSpec((None, None, D_MODEL, 2 * MOE_TF), lambda i, j, be, nv: (l, be[i], 0, ftile(i, j, nv))),
                pl.BlockSpec((None, 1, 2 * MOE_TF), lambda i, j, be, nv: (l * N_EXPERTS + be[i], 0, ftile(i, j, nv))),
                pl.BlockSpec((None, None, MOE_TF, D_MODEL), lambda i, j, be, nv: (l, be[i], ftile(i, j, nv), 0)),
                pl.BlockSpec((None, 1, D_MODEL), lambda i, j, be, nv: (l * N_EXPERTS + be[i], 0, 0)),
            ],
            out_specs=pl.BlockSpec((MOE_BM, D_MODEL), lambda i, j, be, nv: (i, 0)),
            scratch_shapes=[pltpu.VMEM((D_MODEL // LANES, MOE_TF, LANES), F32)],
        ),
        out_shape=jax.ShapeDtypeStruct((MOE_ROWS, D_MODEL), F32),
        compiler_params=_cparams(2),
        name="moe_ffn",
    )(block_e, n_valid, x_disp, w_gu, b_gu.reshape(DEPTH * N_EXPERTS, 1, 2 * D_FF), w_down,
      b_down.reshape(DEPTH * N_EXPERTS, 1, D_MODEL))


CMB_TB = 128
CMB_STEPS = N_TOK // CMB_TB
CMB_ROWS = CMB_TB * TOP_K


def _combine_kernel(dest_ref, y_hbm, gate_ref, x_ref, g2_ref, lng_ref, lnb_ref, o_ref, buf_ref, sem_ref):
    i = pl.program_id(0)
    slot = i % 2

    def row_copy(src_row, s, k, t):
        return pltpu.make_async_copy(y_hbm.at[pl.ds(src_row, 1), :], buf_ref.at[s, k, pl.ds(t, 1), :], sem_ref.at[s])

    def issue(block, s):
        def body(t, carry):
            for k in range(TOP_K):
                row_copy(dest_ref[(block * CMB_TB + t) * TOP_K + k], s, k, t).start()
            return carry
        lax.fori_loop(0, CMB_TB, body, 0)

    @pl.when(i == 0)
    def _():
        issue(0, 0)

    @pl.when(i + 1 < CMB_STEPS)
    def _():
        issue(i + 1, 1 - slot)

    def wait_body(t, carry):
        for k in range(TOP_K):
            row_copy(0, slot, k, t).wait()
        return carry
    lax.fori_loop(0, CMB_TB, wait_body, 0)

    gate = gate_ref[...]
    y = None
    for k in range(TOP_K):
        term = buf_ref[slot, k] * gate[:, k:k + 1]
        y = term if y is None else y + term
    o_ref[...] = _ln_rows(DN_ALPHA * x_ref[...] + g2_ref[...] * y) * lng_ref[...] + lnb_ref[...]


def _moe_combine(y_disp, dest, gate, x, mod3, ln_g, ln_b):
    return pl.pallas_call(
        _combine_kernel,
        grid_spec=pltpu.PrefetchScalarGridSpec(
            num_scalar_prefetch=1,
            grid=(CMB_STEPS,),
            in_specs=[
                pl.BlockSpec(memory_space=pl.ANY),
                pl.BlockSpec((CMB_TB, TOP_K), lambda i, d: (i, 0)),
                pl.BlockSpec((CMB_TB, D_MODEL), lambda i, d: (i, 0)),
                pl.BlockSpec((None, 1, D_MODEL), lambda i, d: (_cond_block(i * CMB_TB), 0, 5)),
                pl.BlockSpec((1, D_MODEL), lambda i, d: (0, 0)),
                pl.BlockSpec((1, D_MODEL), lambda i, d: (0, 0)),
            ],
            out_specs=pl.BlockSpec((CMB_TB, D_MODEL), lambda i, d: (i, 0)),
            scratch_shapes=[pltpu.VMEM((2, TOP_K, CMB_TB, D_MODEL), F32), pltpu.SemaphoreType.DMA((2,))],
        ),
        out_shape=jax.ShapeDtypeStruct((N_TOK, D_MODEL), F32),
        compiler_params=_cparams(1),
        name="moe_combine",
    )(dest, y_disp, gate, x, mod3, ln_g.reshape(1, D_MODEL), ln_b.reshape(1, D_MODEL))


def _moe_route(h, router_w, router_b):
    n_assign = N_TOK * TOP_K
    rw = jnp.pad(router_w, ((0, 0), (0, LANES - N_EXPERTS)))
    rb = jnp.pad(router_b, (0, LANES - N_EXPERTS)).reshape(1, LANES)
    logits = _router_logits(h, rw, rb)[:, :N_EXPERTS]
    top_v, top_i = lax.top_k(logits, TOP_K)
    gate = jax.nn.softmax(top_v, axis=-1)
    e_flat = top_i.reshape(-1)
    order = jnp.argsort(e_flat)
    e_sorted = e_flat[order]
    tok_sorted = (order // TOP_K).astype(jnp.int32)
    counts = jnp.bincount(e_flat, length=N_EXPERTS)
    padded = (counts + MOE_BM - 1) // MOE_BM * MOE_BM
    pad_end = jnp.cumsum(padded)
    pad_start = pad_end - padded
    start = jnp.cumsum(counts) - counts
    dest_sorted = (pad_start[e_sorted] + (jnp.arange(n_assign) - start[e_sorted])).astype(jnp.int32)
    dest = jnp.zeros((n_assign,), jnp.int32).at[order].set(dest_sorted)
    src_tok = jnp.zeros((MOE_ROWS,), jnp.int32).at[dest_sorted].set(tok_sorted)
    n_valid = (pad_end[-1] // MOE_BM).astype(jnp.int32)
    blocks = jnp.arange(MOE_NB, dtype=jnp.int32)
    block_e = jnp.searchsorted(pad_end, jnp.minimum(blocks, n_valid - 1) * MOE_BM, side="right")
    block_e = jnp.minimum(block_e, N_EXPERTS - 1).astype(jnp.int32)
    return gate, dest, src_tok, block_e, n_valid.reshape(1)


def _silu(x):
    return x * jax.nn.sigmoid(x)


def _rope_tables():
    rows = DEC_SEQ // GRID_W
    row = jnp.repeat(jnp.arange(rows, dtype=F32), GRID_W)
    col = jnp.tile(jnp.arange(GRID_W, dtype=F32), rows)
    inv_freq = ROPE_BASE ** (-jnp.arange(ROPE_F, dtype=F32) / ROPE_F)
    ang = jnp.stack([row[:, None] * inv_freq, col[:, None] * inv_freq], axis=1)
    return jnp.cos(ang), jnp.sin(ang)


def _rope_2d(x, cos, sin):
    xs = x.reshape(x.shape[:-1] + (2, 2, ROPE_F))
    a, b = xs[..., 0, :], xs[..., 1, :]
    cs, sn = cos[:, None, None], sin[:, None, None]
    out = jnp.stack([a * cs - b * sn, b * cs + a * sn], axis=-2)
    return out.reshape(x.shape)


def _lam(lam_params, l):
    lam_init = 0.8 - 0.6 * math.exp(-0.3 * l)
    lam = jnp.exp(jnp.sum(lam_params[0] * lam_params[1])) - jnp.exp(jnp.sum(lam_params[2] * lam_params[3])) + lam_init
    return lam, lam_init


def kernel(x_prompt, x_sample, cache_k, cache_v, state_delta, c, c_ctx, w_ada, b_ada, w_in, conv_w, a_log, dt_bias, gdn_norm_w, lam_params, subln_w, w_br, w_gate, b_gate, w_out, ln1_g, ln1_b, ln2_g, ln2_b, router_w, router_b, w_gu, b_gu, w_down, b_down):
    d = D_MODEL
    x = jnp.concatenate([x_prompt.reshape(N_CTX, d), x_sample.reshape(N_LAT, d)], axis=0)

    cond = jnp.concatenate([_silu(c_ctx)[None, :], _silu(c)], axis=0)
    cond = jnp.pad(cond, ((0, N_COND - cond.shape[0]), (0, 0))).astype(BF16)
    rope_cos, rope_sin = _rope_tables()
    w_gate2 = w_gate.reshape(DEPTH, d, N_BRANCH * d)
    w_br2 = w_br.reshape(DEPTH * N_BRANCH, BR_W, d)

    ks, vs, ss = [], [], []
    for l in range(DEPTH):
        mod3 = _mm(cond, w_ada, l, bias=b_ada[l][None, :], bm=N_COND).reshape(N_COND, 1, 6 * d)
        h = _premix(x, mod3)
        w_l = w_in[l]
        w_main = jnp.concatenate([w_l[:, :OFF_Q_B], w_l[:, OFF_Q_B + 4 * H_A:]], axis=1).astype(BF16)
        w_ab = jnp.pad(w_l[:, OFF_Q_B:OFF_Q_B + 4 * H_A], ((0, 0), (0, LANES - 4 * H_A))).astype(BF16)
        proj = _mm(h, w_main[None], 0)
        ab = _mm(h, w_ab[None], 0)[:, :4 * H_A]

        proj_c = proj[:N_CTX].reshape(BATCH, SEQ, MAIN_W)
        proj_s = proj[N_CTX:].reshape(DEC_BATCH, DEC_SEQ, MAIN_W)
        ab_c = ab[:N_CTX].reshape(BATCH, SEQ, 4 * H_A)
        ab_s = ab[N_CTX:].reshape(DEC_BATCH, DEC_SEQ, 4 * H_A)

        oa_c, s_new = _gated_deltanet(proj_c, ab_c, conv_w[l], a_log[l], dt_bias[l], gdn_norm_w[l], None)
        oa_s, _ = _gated_deltanet(proj_s, ab_s, conv_w[l], a_log[l], dt_bias[l], gdn_norm_w[l], state_delta[:, l])

        lam, lam_init = _lam(lam_params[l], l)
        qo, ko, vo = OFF_Q_B // LANES, OFF_K_B // LANES, OFF_V_B // LANES
        ob_c = _diff_attention(proj_c, qo, proj_c, ko, proj_c, vo, lam, subln_w[l], 1.0 - lam_init)
        q_s = proj_s[..., OFF_Q_B:OFF_K_B].reshape(DEC_BATCH, DEC_SEQ, H_B, 2, DH_B)
        k_s = proj_s[..., OFF_K_B:OFF_V_B].reshape(DEC_BATCH, DEC_SEQ, H_B, 2, DH_B)
        q_r = _rope_2d(q_s, rope_cos, rope_sin).reshape(DEC_BATCH, DEC_SEQ, B_W).astype(BF16)
        keys = jnp.concatenate([_rope_2d(k_s, rope_cos, rope_sin).reshape(DEC_BATCH, DEC_SEQ, B_W),
                                cache_k[:, l].reshape(DEC_BATCH, PAST_LEN, B_W)], axis=1).astype(BF16)
        vals = jnp.concatenate([proj_s[..., OFF_V_B:OFF_X_C],
                                cache_v[:, l].reshape(DEC_BATCH, PAST_LEN, B_W)], axis=1).astype(BF16)
        ob_s = _diff_attention(q_r, 0, keys, 0, vals, 0, lam, subln_w[l], 1.0 - lam_init)

        oc_c = _fourier_mix(proj_c, OFF_X_C // C_W)
        oc_s = _fourier_mix(proj_s, OFF_X_C // C_W)

        ks.append(proj_c[..., OFF_K_B:OFF_V_B].reshape(BATCH, SEQ, H_B, 2, DH_B))
        vs.append(proj_c[..., OFF_V_B:OFF_X_C].reshape(BATCH, SEQ, H_B, DV_B))
        ss.append(s_new)

        both = lambda a_c, a_s: jnp.concatenate([a_c.reshape(N_CTX, BR_W), a_s.reshape(N_LAT, BR_W)], axis=0)
        merged = _gated_merge(h, (both(oa_c, oa_s), both(ob_c, ob_s), both(oc_c, oc_s)), w_gate2, b_gate, w_br2, l)
        mix = _mm(merged, w_out, l)

        x, h2, h2_bf = _postmix(x, mix, mod3, ln1_g[l], ln1_b[l])

        gate, dest, src_tok, block_e, n_valid = _moe_route(h2, router_w[l], router_b[l])
        y_disp = _moe_ffn(h2_bf[src_tok], block_e, n_valid, w_gu, b_gu, w_down, b_down, l)
        x = _moe_combine(y_disp, dest, gate, x, mod3, ln2_g[l], ln2_b[l])

    y_prompt = x[:N_CTX].reshape(BATCH, SEQ, d)
    y_sample = x[N_CTX:].reshape(DEC_BATCH, DEC_SEQ, d)
    return (y_prompt, y_sample, jnp.stack(ks, axis=1), jnp.stack(vs, axis=1), jnp.stack(ss, axis=1))
```

```python
import functools
import math

import jax
import jax.numpy as jnp
from jax import lax
from jax.experimental import pallas as pl
from jax.experimental.pallas import tpu as pltpu

F32 = jnp.float32
BF16 = jnp.bfloat16

D_MODEL = 2048
BATCH = 32
SEQ = 256
DEPTH = 4
DEC_BATCH = 2
DEC_SEQ = 1024
PAST_LEN = 512
GRID_W = 64
H_A = 8
DK_A = 128
DV_A = 128
A_W = H_A * DV_A
CHUNK = 64
H_B = 8
DH_B = 64
DV_B = 2 * DH_B
B_W = H_B * DV_B
ROPE_F = DH_B // 4
ROPE_BASE = 10000.0
C_GROUPS = 4
C_GW = 256
C_W = C_GROUPS * C_GW
N_BRANCH = 3
BR_W = 1024
N_EXPERTS = 32
TOP_K = 4
D_FF = 2048
SWIGLU_LIMIT = 7.0
SWIGLU_ALPHA = 1.702
LN_EPS = 1e-5
DN_ALPHA = (2 * DEPTH) ** 0.25

N_CTX = BATCH * SEQ
N_LAT = DEC_BATCH * DEC_SEQ
N_TOK = N_CTX + N_LAT
N_COND = 8

V7X_VMEM_BYTES = 64 * 1024 * 1024
VMEM_LIMIT = V7X_VMEM_BYTES - 8 * 1024 * 1024
LANES = 128

OFF_QKV_A = 0
OFF_Z_A = 3 * A_W
OFF_Q_B = OFF_Z_A + A_W
OFF_K_B = OFF_Q_B + B_W
OFF_V_B = OFF_K_B + B_W
OFF_X_C = OFF_V_B + B_W
MAIN_W = OFF_X_C + C_W


def _cparams(n_axes):
    return pltpu.CompilerParams(dimension_semantics=("arbitrary",) * n_axes, vmem_limit_bytes=VMEM_LIMIT)


def _bdot(a, b):
    return jnp.dot(a.astype(BF16), b.astype(BF16), preferred_element_type=F32)


def _bdot_nt(a, b):
    return lax.dot_general(a.astype(BF16), b.astype(BF16), (((1,), (1,)), ((), ())), preferred_element_type=F32)


def _bdot_tn(a, b):
    return lax.dot_general(a.astype(BF16), b.astype(BF16), (((0,), (0,)), ((), ())), preferred_element_type=F32)


def _cond_block(row0):
    return jnp.where(row0 < N_CTX, 0, 1 + (row0 - N_CTX) // DEC_SEQ)


NORM_TB = 256


def _ln_rows(v):
    mu = jnp.mean(v, axis=-1, keepdims=True)
    var = jnp.mean(jnp.square(v - mu), axis=-1, keepdims=True)
    return (v - mu) * lax.rsqrt(var + LN_EPS)


def _mod_spec(k):
    return pl.BlockSpec((None, 1, D_MODEL), lambda i: (_cond_block(i * NORM_TB), 0, k))


def _premix_kernel(x_ref, sh_ref, sc_ref, o_ref):
    o_ref[...] = (_ln_rows(x_ref[...]) * (1.0 + sc_ref[...]) + sh_ref[...]).astype(o_ref.dtype)


def _premix(x, mod3):
    rows = pl.BlockSpec((NORM_TB, D_MODEL), lambda i: (i, 0))
    return pl.pallas_call(
        _premix_kernel,
        grid=(N_TOK // NORM_TB,),
        in_specs=[rows, _mod_spec(0), _mod_spec(1)],
        out_specs=rows,
        out_shape=jax.ShapeDtypeStruct((N_TOK, D_MODEL), BF16),
        compiler_params=_cparams(1),
        name="premix_norm",
    )(x, mod3, mod3)


def _postmix_kernel(x_ref, mix_ref, g1_ref, lng_ref, lnb_ref, sh_ref, sc_ref, xo_ref, h_ref):
    xn = _ln_rows(DN_ALPHA * x_ref[...] + g1_ref[...] * mix_ref[...]) * lng_ref[...] + lnb_ref[...]
    xo_ref[...] = xn
    h_ref[...] = _ln_rows(xn) * (1.0 + sc_ref[...]) + sh_ref[...]


def _postmix(x, mix, mod3, ln_g, ln_b):
    rows = pl.BlockSpec((NORM_TB, D_MODEL), lambda i: (i, 0))
    vec = pl.BlockSpec((1, D_MODEL), lambda i: (0, 0))
    f32 = jax.ShapeDtypeStruct((N_TOK, D_MODEL), F32)
    return pl.pallas_call(
        _postmix_kernel,
        grid=(N_TOK // NORM_TB,),
        in_specs=[rows, rows, _mod_spec(2), vec, vec, _mod_spec(3), _mod_spec(4)],
        out_specs=[rows, rows],
        out_shape=[f32, f32],
        compiler_params=_cparams(1),
        name="postmix_norm",
    )(x, mix, mod3, ln_g.reshape(1, D_MODEL), ln_b.reshape(1, D_MODEL), mod3, mod3)


def _mm_kernel(x_ref, w_ref, b_ref, o_ref, wbf_ref):
    @pl.when(pl.program_id(1) == 0)
    def _():
        wbf_ref[...] = w_ref[...].astype(BF16)

    acc = jnp.dot(x_ref[...], wbf_ref[...], preferred_element_type=F32) + b_ref[...]
    o_ref[...] = acc.astype(o_ref.dtype)


def _mm(x, w, w_idx, bias=None, out_dtype=F32, bm=512, bn=1024):
    m, k = x.shape
    n = w.shape[-1]
    bm = min(bm, m)
    bn = min(bn, n)
    assert m % bm == 0 and n % bn == 0
    if bias is None:
        bias = jnp.zeros((1, n), F32)
    return pl.pallas_call(
        _mm_kernel,
        grid=(n // bn, m // bm),
        in_specs=[
            pl.BlockSpec((bm, k), lambda j, i: (i, 0)),
            pl.BlockSpec((None, k, bn), lambda j, i: (w_idx, 0, j)),
            pl.BlockSpec((1, bn), lambda j, i: (0, j)),
        ],
        out_specs=pl.BlockSpec((bm, bn), lambda j, i: (i, j)),
        out_shape=jax.ShapeDtypeStruct((m, n), out_dtype),
        scratch_shapes=[pltpu.VMEM((k, bn), BF16)],
        compiler_params=_cparams(2),
        name="dense_mm",
    )(x, w, bias)


def _split_bf16(x):
    hi = x.astype(BF16)
    lo = (x - hi.astype(F32)).astype(BF16)
    return hi, lo


def _router_kernel(x_ref, w_ref, b_ref, o_ref):
    xh, xl = _split_bf16(x_ref[...])
    wh, wl = _split_bf16(w_ref[...])
    acc = jnp.dot(xh, wh, preferred_element_type=F32)
    acc += jnp.dot(xl, wh, preferred_element_type=F32)
    acc += jnp.dot(xh, wl, preferred_element_type=F32)
    o_ref[...] = acc + b_ref[...]


def _router_logits(x, w, b, bm=512):
    m, k = x.shape
    n = w.shape[-1]
    return pl.pallas_call(
        _router_kernel,
        grid=(m // bm,),
        in_specs=[
            pl.BlockSpec((bm, k), lambda i: (i, 0)),
            pl.BlockSpec((k, n), lambda i: (0, 0)),
            pl.BlockSpec((1, n), lambda i: (0, 0)),
        ],
        out_specs=pl.BlockSpec((bm, n), lambda i: (i, 0)),
        out_shape=jax.ShapeDtypeStruct((m, n), F32),
        compiler_params=_cparams(1),
        name="router_logits",
    )(x, w, b)


MERGE_BM = 512
MERGE_BN = 256


def _merge_kernel(h_ref, ba_ref, bb_ref, bc_ref, wg0_ref, wg1_ref, wg2_ref, bg_ref, wb0_ref, wb1_ref, wb2_ref,
                  o_ref, wgc_ref, wbc_ref):
    @pl.when(pl.program_id(1) == 0)
    def _():
        for n, (wg, wb) in enumerate(((wg0_ref, wb0_ref), (wg1_ref, wb1_ref), (wg2_ref, wb2_ref))):
            wgc_ref[n] = wg[...].astype(BF16)
            wbc_ref[n] = wb[...].astype(BF16)

    h = h_ref[...]
    acc = None
    for n, br in enumerate((ba_ref, bb_ref, bc_ref)):
        gate = jax.nn.sigmoid(jnp.dot(h, wgc_ref[n], preferred_element_type=F32) + bg_ref[n:n + 1, :])
        term = gate * jnp.dot(br[...], wbc_ref[n], preferred_element_type=F32)
        acc = term if acc is None else acc + term
    o_ref[...] = acc.astype(o_ref.dtype)


def _gated_merge(h, branches, w_gate2, b_gate, w_br2, l):
    m = h.shape[0]
    nj = D_MODEL // MERGE_BN
    row = lambda width: pl.BlockSpec((MERGE_BM, width), lambda j, i: (i, 0))
    wg = lambda n: pl.BlockSpec((None, D_MODEL, MERGE_BN), lambda j, i: (l, 0, n * nj + j))
    wb = lambda n: pl.BlockSpec((None, BR_W, MERGE_BN), lambda j, i: (l * N_BRANCH + n, 0, j))
    return pl.pallas_call(
        _merge_kernel,
        grid=(nj, m // MERGE_BM),
        in_specs=[row(D_MODEL), row(BR_W), row(BR_W), row(BR_W), wg(0), wg(1), wg(2),
                  pl.BlockSpec((None, N_BRANCH, MERGE_BN), lambda j, i: (l, 0, j)), wb(0), wb(1), wb(2)],
        out_specs=pl.BlockSpec((MERGE_BM, MERGE_BN), lambda j, i: (i, j)),
        out_shape=jax.ShapeDtypeStruct((m, D_MODEL), BF16),
        scratch_shapes=[pltpu.VMEM((N_BRANCH, D_MODEL, MERGE_BN), BF16), pltpu.VMEM((N_BRANCH, BR_W, MERGE_BN), BF16)],
        compiler_params=_cparams(2),
        name="gated_merge",
    )(h, *branches, w_gate2, w_gate2, w_gate2, b_gate, w_br2, w_br2, w_br2)


GDN_HB = 4
GDN_HG = H_A // GDN_HB
GDN_NJ = 2 * GDN_HB
GDN_GC = 4
GDN_GROUP = GDN_GC * CHUNK
GDN_BASE = 4


def _split3(x):
    hi = x.astype(BF16)
    return hi, (x - hi.astype(F32)).astype(BF16)


def _dot3(a_hi, a_lo, b_hi, b_lo):
    lhs = jnp.concatenate([a_hi, a_lo, a_hi], axis=1)
    rhs = jnp.concatenate([b_hi, b_hi, b_lo], axis=0)
    return jnp.dot(lhs, rhs, preferred_element_type=F32)


def _unit_tri_inverse(lmat, eye, base_mask, merge_masks):
    l_hi, l_lo = _split3(lmat)
    zero = jnp.zeros_like(l_hi)
    d_hi = jnp.where(base_mask, l_hi, zero)
    d_lo = jnp.where(base_mask, l_lo, zero)
    d2 = _dot3(d_hi, d_lo, d_hi, d_lo)
    yield
    imd = eye - jnp.where(base_mask, lmat, 0.0)
    tmat = imd + _dot3(*_split3(imd), *_split3(d2))
    yield
    for mask in merge_masks:
        t_hi = tmat.astype(BF16)
        ct = jnp.dot(jnp.where(mask, l_hi, zero), t_hi, preferred_element_type=F32)
        yield
        tmat = tmat - jnp.dot(t_hi, ct.astype(BF16), preferred_element_type=F32)
        yield
    return tmat


def _gdn_kernel(gt_ref, egt_ref, q_ref, k_ref, v_ref, z_ref, cwq_ref, cwk_ref, cwv_ref, gcol_ref, bcol_ref,
                grow_ref, nw_ref, *rest, t_len, has_s0):
    if has_s0:
        s0_ref, o_ref, s_ref, qn_ref, kn_ref, vn_ref, of_ref, ob_ref = rest
    else:
        o_ref, s_ref, qn_ref, kn_ref, vn_ref, of_ref, ob_ref = rest
    n_chunks = t_len // CHUNK
    b = pl.program_id(0)
    hg = pl.program_id(1)

    row = lax.broadcasted_iota(jnp.int32, (t_len, LANES), 0)
    for h in range(GDN_HB):
        cols = slice(h * LANES, (h + 1) * LANES)
        for src, cw, dst, norm in ((q_ref, cwq_ref, qn_ref, True), (k_ref, cwk_ref, kn_ref, True),
                                   (v_ref, cwv_ref, vn_ref, False)):
            x = src[0, :, cols]
            w = cw[:, cols]
            prev = jnp.where(row == 0, 0.0, pltpu.roll(x, 1, 0))
            nxt = jnp.where(row == t_len - 1, 0.0, pltpu.roll(x, t_len - 1, 0))
            y = prev * w[0:1, :] + x * w[1:2, :] + nxt * w[2:3, :]
            y = y * jax.nn.sigmoid(y)
            if norm:
                y = y * lax.rsqrt(jnp.sum(y * y, axis=-1, keepdims=True) + 1e-6)
            dst[:, cols] = y
    if has_s0:
        s_ref[...] = s0_ref[...]
    else:
        s_ref[...] = jnp.zeros_like(s_ref)

    n_groups = t_len // GDN_GROUP
    r_i = lax.broadcasted_iota(jnp.int32, (GDN_GROUP, GDN_GROUP), 0)
    c_i = lax.broadcasted_iota(jnp.int32, (GDN_GROUP, GDN_GROUP), 1)
    same = (r_i // CHUNK) == (c_i // CHUNK)
    eye = (r_i == c_i).astype(F32)
    masks = ((same & (r_i >= c_i), same & (r_i > c_i)), (same & (r_i <= c_i), same & (r_i < c_i)))
    base_mask = (r_i // GDN_BASE) == (c_i // GDN_BASE)
    merge_masks = []
    size = GDN_BASE
    while size < CHUNK:
        merge_masks.append(((r_i // (2 * size)) == (c_i // (2 * size))) & ((r_i // size) != (c_i // size)))
        size *= 2
    scale = DK_A ** -0.5

    def recurrence(n, d, h):
        grp = n if d == 0 else n_groups - 1 - n
        r0 = grp * GDN_GROUP if isinstance(grp, int) else pl.multiple_of(grp * GDN_GROUP, GDN_GROUP)
        incl, strict = masks[d]
        jj = d * GDN_HB + h
        cols = slice(h * LANES, (h + 1) * LANES)
        gcol = gcol_ref[0, 0, grp][:, jj:jj + 1]
        beta = bcol_ref[0, 0, grp][:, jj:jj + 1]
        grow = grow_ref[0, 0, grp][jj:jj + 1, :]
        q_g = qn_ref[pl.ds(r0, GDN_GROUP), cols] * scale
        k_g = kn_ref[pl.ds(r0, GDN_GROUP), cols]
        v_g = vn_ref[pl.ds(r0, GDN_GROUP), cols]
        decay = jnp.where(incl, jnp.exp(jnp.where(incl, gcol - grow, 0.0)), 0.0)
        egc = jnp.exp(gcol)
        kb = k_g * beta
        kq = _bdot_nt(jnp.concatenate([kb, q_g], axis=0), k_g)
        yield
        lmat = jnp.where(strict, kq[:GDN_GROUP] * decay, 0.0)
        attn = jnp.where(incl, kq[GDN_GROUP:] * decay, 0.0)
        tmat = yield from _unit_tri_inverse(lmat, eye, base_mask, merge_masks)
        uw = _bdot(tmat, jnp.concatenate([v_g * beta, kb * egc], axis=1))
        yield
        u, w = uw[:, :DV_A], uw[:, DV_A:]
        qg = q_g * egc
        s_cur = s_ref[0, d, h]
        outs = [None] * GDN_GC
        for cc in (range(GDN_GC) if d == 0 else range(GDN_GC - 1, -1, -1)):
            rows = slice(cc * CHUNK, (cc + 1) * CHUNK)
            sidx = ((b * GDN_HG + hg) * n_chunks + grp * GDN_GC + cc) * GDN_NJ + jj
            ws_qs = _bdot(jnp.concatenate([w[rows], qg[rows]], axis=0), s_cur)
            yield
            v_new = u[rows] - ws_qs[:CHUNK]
            outs[cc] = ws_qs[CHUNK:] + _bdot(attn[rows, rows], v_new)
            k_dec = k_g[rows] * jnp.exp(gt_ref[sidx] - gcol[rows])
            s_cur = s_cur * egt_ref[sidx] + _bdot_tn(k_dec, v_new)
            yield
        return d, h, r0, cols, s_cur, jnp.concatenate(outs, axis=0)

    def group_step(n, carry):
        live = [recurrence(n, d, h) for d in range(2) for h in range(GDN_HB)]
        done = []
        while live:
            for gen in list(live):
                try:
                    next(gen)
                except StopIteration as stop:
                    live.remove(gen)
                    done.append(stop.value)
        for d, h, r0, cols, s_cur, o in done:
            s_ref[0, d, h] = s_cur
            (of_ref, ob_ref)[d][pl.ds(r0, GDN_GROUP), cols] = o
        return carry

    if n_groups == 1:
        group_step(0, 0)
    else:
        lax.fori_loop(0, n_groups, group_step, 0)

    for h in range(GDN_HB):
        cols = slice(h * LANES, (h + 1) * LANES)
        o = of_ref[:, cols] + ob_ref[:, cols]
        z = z_ref[0, :, cols]
        o = o * lax.rsqrt(jnp.mean(o * o, axis=-1, keepdims=True) + 1e-6) * nw_ref[...]
        o_ref[0, :, cols] = (o * (z * jax.nn.sigmoid(z))).astype(o_ref.dtype)


def _gdn_gates(ab, a_log, dt_bias):
    bn, t_len = ab.shape[:2]
    n = t_len // CHUNK
    a_a = ab[..., :2 * H_A].reshape(bn, t_len, 2, H_A)
    b_a = ab[..., 2 * H_A:].reshape(bn, t_len, 2, H_A)
    g = (-jnp.exp(a_log) * jax.nn.softplus(a_a + dt_bias)).reshape(bn, n, CHUNK, 2, H_A)
    beta = jax.nn.sigmoid(b_a).reshape(bn, n, CHUNK, 2, H_A)
    cs_f = jnp.cumsum(g[:, :, :, 0], axis=2)
    cs_b = jnp.flip(jnp.cumsum(jnp.flip(g[:, :, :, 1], axis=2), axis=2), axis=2)
    gcs = jnp.stack([cs_f, cs_b], axis=3)
    g_tot = jnp.stack([cs_f[:, :, -1], cs_b[:, :, 0]], axis=2)

    def group(t):
        t = t.reshape(bn, t_len // GDN_GROUP, GDN_GROUP, 2, GDN_HG, GDN_HB)
        return jnp.transpose(t, (0, 4, 1, 2, 3, 5)).reshape(bn, GDN_HG, t_len // GDN_GROUP, GDN_GROUP, GDN_NJ)

    gcol = group(gcs)
    grow = jnp.swapaxes(gcol, 3, 4)
    g_tot = jnp.transpose(g_tot.reshape(bn, n, 2, GDN_HG, GDN_HB), (0, 3, 1, 2, 4)).reshape(-1)
    return gcol, group(beta), grow, g_tot


def _gated_deltanet(proj, ab, conv_w, a_log, dt_bias, norm_w, s0):
    bn, t_len = proj.shape[:2]
    n = t_len // CHUNK
    gcol, bcol, grow, g_tot = _gdn_gates(ab, a_log, dt_bias)
    gw = GDN_HB * LANES
    off = lambda o: o // gw
    seq = lambda o: pl.BlockSpec((1, t_len, gw), lambda b, g, *_: (b, 0, off(o) + g))
    cw = lambda o: pl.BlockSpec((3, gw), lambda b, g, *_: (0, off(o) + g))
    ng = t_len // GDN_GROUP
    colspec = pl.BlockSpec((1, 1, ng, GDN_GROUP, GDN_NJ), lambda b, g, *_: (b, g, 0, 0, 0))
    state = pl.BlockSpec((1, 2, GDN_HB, DK_A, DV_A), lambda b, g, *_: (b, 0, g, 0, 0))
    in_specs = [seq(OFF_QKV_A), seq(OFF_QKV_A + A_W), seq(OFF_QKV_A + 2 * A_W), seq(OFF_Z_A),
                cw(0), cw(A_W), cw(2 * A_W), colspec, colspec,
                pl.BlockSpec((1, 1, ng, GDN_NJ, GDN_GROUP), lambda b, g, *_: (b, g, 0, 0, 0)),
                pl.BlockSpec((1, LANES), lambda b, g, *_: (0, 0))]
    args = [g_tot, jnp.exp(g_tot), proj, proj, proj, proj, conv_w, conv_w, conv_w, gcol, bcol, grow,
            norm_w.reshape(1, DV_A)]
    if s0 is not None:
        in_specs.append(state)
        args.append(s0)
    return pl.pallas_call(
        functools.partial(_gdn_kernel, t_len=t_len, has_s0=s0 is not None),
        grid_spec=pltpu.PrefetchScalarGridSpec(
            num_scalar_prefetch=2,
            grid=(bn, GDN_HG),
            in_specs=in_specs,
            out_specs=[pl.BlockSpec((1, t_len, gw), lambda b, g, *_: (b, 0, g)), state],
            scratch_shapes=[pltpu.VMEM((t_len, gw), F32)] * 5,
        ),
        out_shape=[jax.ShapeDtypeStruct((bn, t_len, A_W), BF16),
                   jax.ShapeDtypeStruct((bn, 2, H_A, DK_A, DV_A), F32)],
        compiler_params=_cparams(2),
        name="gated_deltanet",
    )(*args)


def _attn_kernel(lam_ref, q_ref, k_ref, v_ref, w_ref, o_ref, *, scale, post_scale):
    q = q_ref[0].astype(BF16)
    k = k_ref[0].astype(BF16)
    v = v_ref[0].astype(BF16)
    lam = lam_ref[0]

    def softmax_map(qm, km):
        s = lax.dot_general(qm, km, (((1,), (1,)), ((), ())), preferred_element_type=F32) * scale
        e = jnp.exp(s - jnp.max(s, axis=-1, keepdims=True))
        return e / jnp.sum(e, axis=-1, keepdims=True)

    p1 = softmax_map(q[:, :DH_B], k[:, :DH_B])
    p2 = softmax_map(q[:, DH_B:], k[:, DH_B:])
    a = (p1 - lam * p2).astype(BF16)
    o = jnp.dot(a, v, preferred_element_type=F32)
    o = o * lax.rsqrt(jnp.mean(o * o, axis=-1, keepdims=True) + 1e-6)
    o_ref[0] = (o * w_ref[...] * post_scale).astype(o_ref.dtype)


def _diff_attention(q, q_off, k, k_off, v, v_off, lam, subln_w, post_scale, bq=256):
    bn, tq = q.shape[:2]
    tk = k.shape[1]
    return pl.pallas_call(
        functools.partial(_attn_kernel, scale=DH_B ** -0.5, post_scale=post_scale),
        grid=(bn, H_B, tq // bq),
        in_specs=[
            pl.BlockSpec(memory_space=pltpu.SMEM),
            pl.BlockSpec((1, bq, LANES), lambda b, h, i: (b, i, q_off + h)),
            pl.BlockSpec((1, tk, LANES), lambda b, h, i: (b, 0, k_off + h)),
            pl.BlockSpec((1, tk, LANES), lambda b, h, i: (b, 0, v_off + h)),
            pl.BlockSpec((1, LANES), lambda b, h, i: (0, 0)),
        ],
        out_specs=pl.BlockSpec((1, bq, LANES), lambda b, h, i: (b, i, h)),
        out_shape=jax.ShapeDtypeStruct((bn, tq, B_W), BF16),
        compiler_params=_cparams(3),
        name="diff_attention",
    )(lam.reshape(1), q, k, v, subln_w.reshape(1, DV_B))


def _dft_mats(n):
    j = lax.broadcasted_iota(jnp.int32, (n, n), 0)
    k = lax.broadcasted_iota(jnp.int32, (n, n), 1)
    ang = ((j * k) % n).astype(F32) * (2.0 * math.pi / n)
    s = n ** -0.5
    return (jnp.cos(ang) * s).astype(BF16), (jnp.sin(ang) * s).astype(BF16)


def _dft_kernel(x_ref, cc_ref, sc_ref, ct_ref, st_ref, o_ref):
    x = x_ref[0].astype(BF16)
    for g in range(C_GROUPS):
        xg = x[:, g * C_GW:(g + 1) * C_GW]
        a = jnp.dot(xg, cc_ref[...], preferred_element_type=F32).astype(BF16)
        b = jnp.dot(xg, sc_ref[...], preferred_element_type=F32).astype(BF16)
        o = jnp.dot(ct_ref[...], a, preferred_element_type=F32) - jnp.dot(st_ref[...], b, preferred_element_type=F32)
        o_ref[0, :, g * C_GW:(g + 1) * C_GW] = o.astype(o_ref.dtype)


def _fourier_mix(x, x_off):
    bn, t = x.shape[:2]
    cc, sc = _dft_mats(C_GW)
    ct, st = _dft_mats(t)
    full = lambda shape: pl.BlockSpec(shape, lambda b: (0, 0))
    return pl.pallas_call(
        _dft_kernel,
        grid=(bn,),
        in_specs=[
            pl.BlockSpec((1, t, C_W), lambda b: (b, 0, x_off)),
            full((C_GW, C_GW)), full((C_GW, C_GW)), full((t, t)), full((t, t)),
        ],
        out_specs=pl.BlockSpec((1, t, C_W), lambda b: (b, 0, 0)),
        out_shape=jax.ShapeDtypeStruct((bn, t, C_W), BF16),
        compiler_params=_cparams(1),
        name="fourier_mix",
    )(x, cc, sc, ct, st)


MOE_BM = 512
MOE_TF = 512
MOE_NF = D_FF // MOE_TF
MOE_ROWS = -(-(N_TOK * TOP_K + N_EXPERTS * (MOE_BM - 1)) // MOE_BM) * MOE_BM
MOE_NB = MOE_ROWS // MOE_BM


def _moe_kernel(be_ref, nv_ref, x_ref, wgu_ref, bgu_ref, wd_ref, bd_ref, o_ref, wdp_ref):
    i = pl.program_id(0)
    j = pl.program_id(1)
    half = MOE_TF // 2
    n_chunk = MOE_TF // LANES

    @pl.when(i < nv_ref[0])
    def _():
        hb = jnp.dot(x_ref[...].astype(BF16), wgu_ref[...].astype(BF16), preferred_element_type=F32) + bgu_ref[...]
        even = (lax.broadcasted_iota(jnp.int32, (MOE_BM, LANES), 1) % 2) == 0
        acts = []
        for c in range(n_chunk):
            lo = hb[:, c * LANES:(c + 1) * LANES]
            hi = hb[:, (c + n_chunk) * LANES:(c + n_chunk + 1) * LANES]
            gate = jnp.where(even, lo, pltpu.roll(hi, 1, 1))
            up = jnp.where(even, pltpu.roll(lo, LANES - 1, 1), hi)
            gate = jnp.minimum(gate, SWIGLU_LIMIT)
            up = jnp.clip(up, -SWIGLU_LIMIT, SWIGLU_LIMIT)
            acts.append(((up + 1.0) * gate * jax.nn.sigmoid(SWIGLU_ALPHA * gate)).astype(BF16))
        act = jnp.concatenate(acts, axis=1)
        for cb in range(D_MODEL // LANES):
            cols = slice(cb * LANES, (cb + 1) * LANES)
            wdp_ref[cb, pl.ds(0, half, stride=2), :] = wd_ref[0:half, cols]
            wdp_ref[cb, pl.ds(1, half, stride=2), :] = wd_ref[half:MOE_TF, cols]
        wdp = jnp.concatenate([wdp_ref[cb].astype(BF16) for cb in range(D_MODEL // LANES)], axis=1)
        contrib = jnp.dot(act, wdp, preferred_element_type=F32)

        @pl.when(j == 0)
        def _():
            o_ref[...] = contrib + bd_ref[...]

        @pl.when(j > 0)
        def _():
            o_ref[...] += contrib

    @pl.when((i >= nv_ref[0]) & (j == 0))
    def _():
        o_ref[...] = jnp.zeros_like(o_ref)


def _moe_ffn(x_disp, block_e, n_valid, w_gu, b_gu, w_down, b_down, l):
    def blk(i, nv):
        return jnp.minimum(i, nv[0] - 1)

    def ftile(i, j, nv):
        return jnp.where(i < nv[0], j, MOE_NF - 1)

    return pl.pallas_call(
        _moe_kernel,
        grid_spec=pltpu.PrefetchScalarGridSpec(
            num_scalar_prefetch=2,
            grid=(MOE_NB, MOE_NF),
            in_specs=[
                pl.BlockSpec((MOE_BM, D_MODEL), lambda i, j, be, nv: (blk(i, nv), 0)),
                pl.BlockSpec((None, None, D_MODEL, 2 * MOE_TF), lambda i, j, be, nv: (l, be[i], 0, ftile(i, j, nv))),
                pl.BlockSpec((None, 1, 2 * MOE_TF), lambda i, j, be, nv: (l * N_EXPERTS + be[i], 0, ftile(i, j, nv))),
                pl.BlockSpec((None, None, MOE_TF, D_MODEL), lambda i, j, be, nv: (l, be[i], ftile(i, j, nv), 0)),
                pl.BlockSpec((None, 1, D_MODEL), lambda i, j, be, nv: (l * N_EXPERTS + be[i], 0, 0)),
            ],
            out_specs=pl.BlockSpec((MOE_BM, D_MODEL), lambda i, j, be, nv: (i, 0)),
            scratch_shapes=[pltpu.VMEM((D_MODEL // LANES, MOE_TF, LANES), F32)],
        ),
        out_shape=jax.ShapeDtypeStruct((MOE_ROWS, D_MODEL), F32),
        compiler_params=_cparams(2),
        name="moe_ffn",
    )(block_e, n_valid, x_disp, w_gu, b_gu.reshape(DEPTH * N_EXPERTS, 1, 2 * D_FF), w_down,
      b_down.reshape(DEPTH * N_EXPERTS, 1, D_MODEL))


CMB_TB = 128
CMB_STEPS = N_TOK // CMB_TB
CMB_ROWS = CMB_TB * TOP_K


def _combine_kernel(dest_ref, y_hbm, gate_ref, x_ref, g2_ref, lng_ref, lnb_ref, o_ref, buf_ref, sem_ref):
    i = pl.program_id(0)
    slot = i % 2

    def row_copy(src_row, s, k, t):
        return pltpu.make_async_copy(y_hbm.at[pl.ds(src_row, 1), :], buf_ref.at[s, k, pl.ds(t, 1), :], sem_ref.at[s])

    def issue(block, s):
        def body(t, carry):
            for k in range(TOP_K):
                row_copy(dest_ref[(block * CMB_TB + t) * TOP_K + k], s, k, t).start()
            return carry
        lax.fori_loop(0, CMB_TB, body, 0)

    @pl.when(i == 0)
    def _():
        issue(0, 0)

    @pl.when(i + 1 < CMB_STEPS)
    def _():
        issue(i + 1, 1 - slot)

    def wait_body(t, carry):
        for k in range(TOP_K):
            row_copy(0, slot, k, t).wait()
        return carry
    lax.fori_loop(0, CMB_TB, wait_body, 0)

    gate = gate_ref[...]
    y = None
    for k in range(TOP_K):
        term = buf_ref[slot, k] * gate[:, k:k + 1]
        y = term if y is None else y + term
    o_ref[...] = _ln_rows(DN_ALPHA * x_ref[...] + g2_ref[...] * y) * lng_ref[...] + lnb_ref[...]


def _moe_combine(y_disp, dest, gate, x, mod3, ln_g, ln_b):
    return pl.pallas_call(
        _combine_kernel,
        grid_spec=pltpu.PrefetchScalarGridSpec(
            num_scalar_prefetch=1,
            grid=(CMB_STEPS,),
            in_specs=[
                pl.BlockSpec(memory_space=pl.ANY),
                pl.BlockSpec((CMB_TB, TOP_K), lambda i, d: (i, 0)),
                pl.BlockSpec((CMB_TB, D_MODEL), lambda i, d: (i, 0)),
                pl.BlockSpec((None, 1, D_MODEL), lambda i, d: (_cond_block(i * CMB_TB), 0, 5)),
                pl.BlockSpec((1, D_MODEL), lambda i, d: (0, 0)),
                pl.BlockSpec((1, D_MODEL), lambda i, d: (0, 0)),
            ],
            out_specs=pl.BlockSpec((CMB_TB, D_MODEL), lambda i, d: (i, 0)),
            scratch_shapes=[pltpu.VMEM((2, TOP_K, CMB_TB, D_MODEL), F32), pltpu.SemaphoreType.DMA((2,))],
        ),
        out_shape=jax.ShapeDtypeStruct((N_TOK, D_MODEL), F32),
        compiler_params=_cparams(1),
        name="moe_combine",
    )(dest, y_disp, gate, x, mod3, ln_g.reshape(1, D_MODEL), ln_b.reshape(1, D_MODEL))


def _moe_route(h, router_w, router_b):
    n_assign = N_TOK * TOP_K
    rw = jnp.pad(router_w, ((0, 0), (0, LANES - N_EXPERTS)))
    rb = jnp.pad(router_b, (0, LANES - N_EXPERTS)).reshape(1, LANES)
    logits = _router_logits(h, rw, rb)[:, :N_EXPERTS]
    top_v, top_i = lax.top_k(logits, TOP_K)
    gate = jax.nn.softmax(top_v, axis=-1)
    e_flat = top_i.reshape(-1)
    order = jnp.argsort(e_flat)
    e_sorted = e_flat[order]
    tok_sorted = (order // TOP_K).astype(jnp.int32)
    counts = jnp.bincount(e_flat, length=N_EXPERTS)
    padded = (counts + MOE_BM - 1) // MOE_BM * MOE_BM
    pad_end = jnp.cumsum(padded)
    pad_start = pad_end - padded
    start = jnp.cumsum(counts) - counts
    dest_sorted = (pad_start[e_sorted] + (jnp.arange(n_assign) - start[e_sorted])).astype(jnp.int32)
    dest = jnp.zeros((n_assign,), jnp.int32).at[order].set(dest_sorted)
    src_tok = jnp.zeros((MOE_ROWS,), jnp.int32).at[dest_sorted].set(tok_sorted)
    n_valid = (pad_end[-1] // MOE_BM).astype(jnp.int32)
    blocks = jnp.arange(MOE_NB, dtype=jnp.int32)
    block_e = jnp.searchsorted(pad_end, jnp.minimum(blocks, n_valid - 1) * MOE_BM, side="right")
    block_e = jnp.minimum(block_e, N_EXPERTS - 1).astype(jnp.int32)
    return gate, dest, src_tok, block_e, n_valid.reshape(1)


def _silu(x):
    return x * jax.nn.sigmoid(x)


def _rope_tables():
    rows = DEC_SEQ // GRID_W
    row = jnp.repeat(jnp.arange(rows, dtype=F32), GRID_W)
    col = jnp.tile(jnp.arange(GRID_W, dtype=F32), rows)
    inv_freq = ROPE_BASE ** (-jnp.arange(ROPE_F, dtype=F32) / ROPE_F)
    ang = jnp.stack([row[:, None] * inv_freq, col[:, None] * inv_freq], axis=1)
    return jnp.cos(ang), jnp.sin(ang)


def _rope_2d(x, cos, sin):
    xs = x.reshape(x.shape[:-1] + (2, 2, ROPE_F))
    a, b = xs[..., 0, :], xs[..., 1, :]
    cs, sn = cos[:, None, None], sin[:, None, None]
    out = jnp.stack([a * cs - b * sn, b * cs + a * sn], axis=-2)
    return out.reshape(x.shape)


def _lam(lam_params, l):
    lam_init = 0.8 - 0.6 * math.exp(-0.3 * l)
    lam = jnp.exp(jnp.sum(lam_params[0] * lam_params[1])) - jnp.exp(jnp.sum(lam_params[2] * lam_params[3])) + lam_init
    return lam, lam_init


def kernel(x_prompt, x_sample, cache_k, cache_v, state_delta, c, c_ctx, w_ada, b_ada, w_in, conv_w, a_log, dt_bias, gdn_norm_w, lam_params, subln_w, w_br, w_gate, b_gate, w_out, ln1_g, ln1_b, ln2_g, ln2_b, router_w, router_b, w_gu, b_gu, w_down, b_down):
    d = D_MODEL
    x = jnp.concatenate([x_prompt.reshape(N_CTX, d), x_sample.reshape(N_LAT, d)], axis=0)

    cond = jnp.concatenate([_silu(c_ctx)[None, :], _silu(c)], axis=0)
    cond = jnp.pad(cond, ((0, N_COND - cond.shape[0]), (0, 0))).astype(BF16)
    rope_cos, rope_sin = _rope_tables()
    w_gate2 = w_gate.reshape(DEPTH, d, N_BRANCH * d)
    w_br2 = w_br.reshape(DEPTH * N_BRANCH, BR_W, d)

    ks, vs, ss = [], [], []
    for l in range(DEPTH):
        mod3 = _mm(cond, w_ada, l, bias=b_ada[l][None, :], bm=N_COND).reshape(N_COND, 1, 6 * d)
        h = _premix(x, mod3)
        w_l = w_in[l]
        w_main = jnp.concatenate([w_l[:, :OFF_Q_B], w_l[:, OFF_Q_B + 4 * H_A:]], axis=1).astype(BF16)
        w_ab = jnp.pad(w_l[:, OFF_Q_B:OFF_Q_B + 4 * H_A], ((0, 0), (0, LANES - 4 * H_A))).astype(BF16)
        proj = _mm(h, w_main[None], 0)
        ab = _mm(h, w_ab[None], 0)[:, :4 * H_A]

        proj_c = proj[:N_CTX].reshape(BATCH, SEQ, MAIN_W)
        proj_s = proj[N_CTX:].reshape(DEC_BATCH, DEC_SEQ, MAIN_W)
        ab_c = ab[:N_CTX].reshape(BATCH, SEQ, 4 * H_A)
        ab_s = ab[N_CTX:].reshape(DEC_BATCH, DEC_SEQ, 4 * H_A)

        oa_c, s_new = _gated_deltanet(proj_c, ab_c, conv_w[l], a_log[l], dt_bias[l], gdn_norm_w[l], None)
        oa_s, _ = _gated_deltanet(proj_s, ab_s, conv_w[l], a_log[l], dt_bias[l], gdn_norm_w[l], state_delta[:, l])

        lam, lam_init = _lam(lam_params[l], l)
        qo, ko, vo = OFF_Q_B // LANES, OFF_K_B // LANES, OFF_V_B // LANES
        ob_c = _diff_attention(proj_c, qo, proj_c, ko, proj_c, vo, lam, subln_w[l], 1.0 - lam_init)
        q_s = proj_s[..., OFF_Q_B:OFF_K_B].reshape(DEC_BATCH, DEC_SEQ, H_B, 2, DH_B)
        k_s = proj_s[..., OFF_K_B:OFF_V_B].reshape(DEC_BATCH, DEC_SEQ, H_B, 2, DH_B)
        q_r = _rope_2d(q_s, rope_cos, rope_sin).reshape(DEC_BATCH, DEC_SEQ, B_W).astype(BF16)
        keys = jnp.concatenate([_rope_2d(k_s, rope_cos, rope_sin).reshape(DEC_BATCH, DEC_SEQ, B_W),
                                cache_k[:, l].reshape(DEC_BATCH, PAST_LEN, B_W)], axis=1).astype(BF16)
        vals = jnp.concatenate([proj_s[..., OFF_V_B:OFF_X_C],
                                cache_v[:, l].reshape(DEC_BATCH, PAST_LEN, B_W)], axis=1).astype(BF16)
        ob_s = _diff_attention(q_r, 0, keys, 0, vals, 0, lam, subln_w[l], 1.0 - lam_init)

        oc_c = _fourier_mix(proj_c, OFF_X_C // C_W)
        oc_s = _fourier_mix(proj_s, OFF_X_C // C_W)

        ks.append(proj_c[..., OFF_K_B:OFF_V_B].reshape(BATCH, SEQ, H_B, 2, DH_B))
        vs.append(proj_c[..., OFF_V_B:OFF_X_C].reshape(BATCH, SEQ, H_B, DV_B))
        ss.append(s_new)

        both = lambda a_c, a_s: jnp.concatenate([a_c.reshape(N_CTX, BR_W), a_s.reshape(N_LAT, BR_W)], axis=0)
        merged = _gated_merge(h, (both(oa_c, oa_s), both(ob_c, ob_s), both(oc_c, oc_s)), w_gate2, b_gate, w_br2, l)
        mix = _mm(merged, w_out, l)

        x, h2 = _postmix(x, mix, mod3, ln1_g[l], ln1_b[l])

        gate, dest, src_tok, block_e, n_valid = _moe_route(h2, router_w[l], router_b[l])
        y_disp = _moe_ffn(h2[src_tok], block_e, n_valid, w_gu, b_gu, w_down, b_down, l)
        x = _moe_combine(y_disp, dest, gate, x, mod3, ln2_g[l], ln2_b[l])

    y_prompt = x[:N_CTX].reshape(BATCH, SEQ, d)
    y_sample = x[N_CTX:].reshape(DEC_BATCH, DEC_SEQ, d)
    return (y_prompt, y_sample, jnp.stack(ks, axis=1), jnp.stack(vs, axis=1), jnp.stack(ss, axis=1))
```

```python
import functools
import math

import jax
import jax.numpy as jnp
from jax import lax
from jax.experimental import pallas as pl
from jax.experimental.pallas import tpu as pltpu

F32 = jnp.float32
BF16 = jnp.bfloat16

D_MODEL = 2048
BATCH = 32
SEQ = 256
DEPTH = 4
DEC_BATCH = 2
DEC_SEQ = 1024
PAST_LEN = 512
GRID_W = 64
H_A = 8
DK_A = 128
DV_A = 128
A_W = H_A * DV_A
CHUNK = 64
H_B = 8
DH_B = 64
DV_B = 2 * DH_B
B_W = H_B * DV_B
ROPE_F = DH_B // 4
ROPE_BASE = 10000.0
C_GROUPS = 4
C_GW = 256
C_W = C_GROUPS * C_GW
N_BRANCH = 3
BR_W = 1024
N_EXPERTS = 32
TOP_K = 4
D_FF = 2048
SWIGLU_LIMIT = 7.0
SWIGLU_ALPHA = 1.702
LN_EPS = 1e-5
DN_ALPHA = (2 * DEPTH) ** 0.25

N_CTX = BATCH * SEQ
N_LAT = DEC_BATCH * DEC_SEQ
N_TOK = N_CTX + N_LAT
N_COND = 8

V7X_VMEM_BYTES = 64 * 1024 * 1024
VMEM_LIMIT = V7X_VMEM_BYTES - 8 * 1024 * 1024
LANES = 128

OFF_QKV_A = 0
OFF_Z_A = 3 * A_W
OFF_Q_B = OFF_Z_A + A_W
OFF_K_B = OFF_Q_B + B_W
OFF_V_B = OFF_K_B + B_W
OFF_X_C = OFF_V_B + B_W
MAIN_W = OFF_X_C + C_W


def _cparams(n_axes):
    return pltpu.CompilerParams(dimension_semantics=("arbitrary",) * n_axes, vmem_limit_bytes=VMEM_LIMIT)


def _bdot(a, b):
    return jnp.dot(a.astype(BF16), b.astype(BF16), preferred_element_type=F32)


def _bdot_nt(a, b):
    return lax.dot_general(a.astype(BF16), b.astype(BF16), (((1,), (1,)), ((), ())), preferred_element_type=F32)


def _bdot_tn(a, b):
    return lax.dot_general(a.astype(BF16), b.astype(BF16), (((0,), (0,)), ((), ())), preferred_element_type=F32)


def _cond_block(row0):
    return jnp.where(row0 < N_CTX, 0, 1 + (row0 - N_CTX) // DEC_SEQ)


NORM_TB = 256


def _ln_rows(v):
    mu = jnp.mean(v, axis=-1, keepdims=True)
    var = jnp.mean(jnp.square(v - mu), axis=-1, keepdims=True)
    return (v - mu) * lax.rsqrt(var + LN_EPS)


def _mod_spec(k):
    return pl.BlockSpec((None, 1, D_MODEL), lambda i: (_cond_block(i * NORM_TB), 0, k))


def _premix_kernel(x_ref, sh_ref, sc_ref, o_ref):
    o_ref[...] = (_ln_rows(x_ref[...]) * (1.0 + sc_ref[...]) + sh_ref[...]).astype(o_ref.dtype)


def _premix(x, mod3):
    rows = pl.BlockSpec((NORM_TB, D_MODEL), lambda i: (i, 0))
    return pl.pallas_call(
        _premix_kernel,
        grid=(N_TOK // NORM_TB,),
        in_specs=[rows, _mod_spec(0), _mod_spec(1)],
        out_specs=rows,
        out_shape=jax.ShapeDtypeStruct((N_TOK, D_MODEL), BF16),
        compiler_params=_cparams(1),
        name="premix_norm",
    )(x, mod3, mod3)


def _postmix_kernel(x_ref, mix_ref, g1_ref, lng_ref, lnb_ref, sh_ref, sc_ref, xo_ref, h_ref):
    xn = _ln_rows(DN_ALPHA * x_ref[...] + g1_ref[...] * mix_ref[...]) * lng_ref[...] + lnb_ref[...]
    xo_ref[...] = xn
    h_ref[...] = _ln_rows(xn) * (1.0 + sc_ref[...]) + sh_ref[...]


def _postmix(x, mix, mod3, ln_g, ln_b):
    rows = pl.BlockSpec((NORM_TB, D_MODEL), lambda i: (i, 0))
    vec = pl.BlockSpec((1, D_MODEL), lambda i: (0, 0))
    f32 = jax.ShapeDtypeStruct((N_TOK, D_MODEL), F32)
    return pl.pallas_call(
        _postmix_kernel,
        grid=(N_TOK // NORM_TB,),
        in_specs=[rows, rows, _mod_spec(2), vec, vec, _mod_spec(3), _mod_spec(4)],
        out_specs=[rows, rows],
        out_shape=[f32, f32],
        compiler_params=_cparams(1),
        name="postmix_norm",
    )(x, mix, mod3, ln_g.reshape(1, D_MODEL), ln_b.reshape(1, D_MODEL), mod3, mod3)


def _mm_kernel(x_ref, w_ref, b_ref, o_ref, wbf_ref):
    @pl.when(pl.program_id(1) == 0)
    def _():
        wbf_ref[...] = w_ref[...].astype(BF16)

    acc = jnp.dot(x_ref[...], wbf_ref[...], preferred_element_type=F32) + b_ref[...]
    o_ref[...] = acc.astype(o_ref.dtype)


def _mm(x, w, w_idx, bias=None, out_dtype=F32, bm=512, bn=1024):
    m, k = x.shape
    n = w.shape[-1]
    bm = min(bm, m)
    bn = min(bn, n)
    assert m % bm == 0 and n % bn == 0
    if bias is None:
        bias = jnp.zeros((1, n), F32)
    return pl.pallas_call(
        _mm_kernel,
        grid=(n // bn, m // bm),
        in_specs=[
            pl.BlockSpec((bm, k), lambda j, i: (i, 0)),
            pl.BlockSpec((None, k, bn), lambda j, i: (w_idx, 0, j)),
            pl.BlockSpec((1, bn), lambda j, i: (0, j)),
        ],
        out_specs=pl.BlockSpec((bm, bn), lambda j, i: (i, j)),
        out_shape=jax.ShapeDtypeStruct((m, n), out_dtype),
        scratch_shapes=[pltpu.VMEM((k, bn), BF16)],
        compiler_params=_cparams(2),
        name="dense_mm",
    )(x, w, bias)


def _split_bf16(x):
    hi = x.astype(BF16)
    lo = (x - hi.astype(F32)).astype(BF16)
    return hi, lo


def _router_kernel(x_ref, w_ref, b_ref, o_ref):
    xh, xl = _split_bf16(x_ref[...])
    wh, wl = _split_bf16(w_ref[...])
    acc = jnp.dot(xh, wh, preferred_element_type=F32)
    acc += jnp.dot(xl, wh, preferred_element_type=F32)
    acc += jnp.dot(xh, wl, preferred_element_type=F32)
    o_ref[...] = acc + b_ref[...]


def _router_logits(x, w, b, bm=512):
    m, k = x.shape
    n = w.shape[-1]
    return pl.pallas_call(
        _router_kernel,
        grid=(m // bm,),
        in_specs=[
            pl.BlockSpec((bm, k), lambda i: (i, 0)),
            pl.BlockSpec((k, n), lambda i: (0, 0)),
            pl.BlockSpec((1, n), lambda i: (0, 0)),
        ],
        out_specs=pl.BlockSpec((bm, n), lambda i: (i, 0)),
        out_shape=jax.ShapeDtypeStruct((m, n), F32),
        compiler_params=_cparams(1),
        name="router_logits",
    )(x, w, b)


MERGE_BM = 512
MERGE_BN = 256


def _merge_kernel(h_ref, ba_ref, bb_ref, bc_ref, wg0_ref, wg1_ref, wg2_ref, bg_ref, wb0_ref, wb1_ref, wb2_ref,
                  o_ref, wgc_ref, wbc_ref):
    @pl.when(pl.program_id(1) == 0)
    def _():
        for n, (wg, wb) in enumerate(((wg0_ref, wb0_ref), (wg1_ref, wb1_ref), (wg2_ref, wb2_ref))):
            wgc_ref[n] = wg[...].astype(BF16)
            wbc_ref[n] = wb[...].astype(BF16)

    h = h_ref[...]
    acc = None
    for n, br in enumerate((ba_ref, bb_ref, bc_ref)):
        gate = jax.nn.sigmoid(jnp.dot(h, wgc_ref[n], preferred_element_type=F32) + bg_ref[n:n + 1, :])
        term = gate * jnp.dot(br[...], wbc_ref[n], preferred_element_type=F32)
        acc = term if acc is None else acc + term
    o_ref[...] = acc.astype(o_ref.dtype)


def _gated_merge(h, branches, w_gate2, b_gate, w_br2, l):
    m = h.shape[0]
    nj = D_MODEL // MERGE_BN
    row = lambda width: pl.BlockSpec((MERGE_BM, width), lambda j, i: (i, 0))
    wg = lambda n: pl.BlockSpec((None, D_MODEL, MERGE_BN), lambda j, i: (l, 0, n * nj + j))
    wb = lambda n: pl.BlockSpec((None, BR_W, MERGE_BN), lambda j, i: (l * N_BRANCH + n, 0, j))
    return pl.pallas_call(
        _merge_kernel,
        grid=(nj, m // MERGE_BM),
        in_specs=[row(D_MODEL), row(BR_W), row(BR_W), row(BR_W), wg(0), wg(1), wg(2),
                  pl.BlockSpec((None, N_BRANCH, MERGE_BN), lambda j, i: (l, 0, j)), wb(0), wb(1), wb(2)],
        out_specs=pl.BlockSpec((MERGE_BM, MERGE_BN), lambda j, i: (i, j)),
        out_shape=jax.ShapeDtypeStruct((m, D_MODEL), BF16),
        scratch_shapes=[pltpu.VMEM((N_BRANCH, D_MODEL, MERGE_BN), BF16), pltpu.VMEM((N_BRANCH, BR_W, MERGE_BN), BF16)],
        compiler_params=_cparams(2),
        name="gated_merge",
    )(h, *branches, w_gate2, w_gate2, w_gate2, b_gate, w_br2, w_br2, w_br2)


GDN_HB = 4
GDN_HG = H_A // GDN_HB
GDN_NJ = 2 * GDN_HB
GDN_GC = 4
GDN_GROUP = GDN_GC * CHUNK
GDN_BASE = 4


def _split3(x):
    hi = x.astype(BF16)
    return hi, (x - hi.astype(F32)).astype(BF16)


def _dot3(a_hi, a_lo, b_hi, b_lo):
    lhs = jnp.concatenate([a_hi, a_lo, a_hi], axis=1)
    rhs = jnp.concatenate([b_hi, b_hi, b_lo], axis=0)
    return jnp.dot(lhs, rhs, preferred_element_type=F32)


def _unit_tri_inverse(lmat, eye, base_mask, merge_masks):
    l_hi, l_lo = _split3(lmat)
    zero = jnp.zeros_like(l_hi)
    d_hi = jnp.where(base_mask, l_hi, zero)
    d_lo = jnp.where(base_mask, l_lo, zero)
    d2 = _dot3(d_hi, d_lo, d_hi, d_lo)
    yield
    imd = eye - jnp.where(base_mask, lmat, 0.0)
    tmat = imd + _dot3(*_split3(imd), *_split3(d2))
    yield
    for mask in merge_masks:
        t_hi = tmat.astype(BF16)
        ct = jnp.dot(jnp.where(mask, l_hi, zero), t_hi, preferred_element_type=F32)
        yield
        tmat = tmat - jnp.dot(t_hi, ct.astype(BF16), preferred_element_type=F32)
        yield
    return tmat


def _gdn_kernel(gt_ref, egt_ref, q_ref, k_ref, v_ref, z_ref, cwq_ref, cwk_ref, cwv_ref, gcol_ref, bcol_ref,
                grow_ref, nw_ref, *rest, t_len, has_s0):
    if has_s0:
        s0_ref, o_ref, s_ref, qn_ref, kn_ref, vn_ref, of_ref, ob_ref = rest
    else:
        o_ref, s_ref, qn_ref, kn_ref, vn_ref, of_ref, ob_ref = rest
    n_chunks = t_len // CHUNK
    b = pl.program_id(0)
    hg = pl.program_id(1)

    row = lax.broadcasted_iota(jnp.int32, (t_len, LANES), 0)
    for h in range(GDN_HB):
        cols = slice(h * LANES, (h + 1) * LANES)
        for src, cw, dst, norm in ((q_ref, cwq_ref, qn_ref, True), (k_ref, cwk_ref, kn_ref, True),
                                   (v_ref, cwv_ref, vn_ref, False)):
            x = src[0, :, cols]
            w = cw[:, cols]
            prev = jnp.where(row == 0, 0.0, pltpu.roll(x, 1, 0))
            nxt = jnp.where(row == t_len - 1, 0.0, pltpu.roll(x, t_len - 1, 0))
            y = prev * w[0:1, :] + x * w[1:2, :] + nxt * w[2:3, :]
            y = y * jax.nn.sigmoid(y)
            if norm:
                y = y * lax.rsqrt(jnp.sum(y * y, axis=-1, keepdims=True) + 1e-6)
            dst[:, cols] = y
    if has_s0:
        s_ref[...] = s0_ref[...]
    else:
        s_ref[...] = jnp.zeros_like(s_ref)

    n_groups = t_len // GDN_GROUP
    r_i = lax.broadcasted_iota(jnp.int32, (GDN_GROUP, GDN_GROUP), 0)
    c_i = lax.broadcasted_iota(jnp.int32, (GDN_GROUP, GDN_GROUP), 1)
    same = (r_i // CHUNK) == (c_i // CHUNK)
    eye = (r_i == c_i).astype(F32)
    masks = ((same & (r_i >= c_i), same & (r_i > c_i)), (same & (r_i <= c_i), same & (r_i < c_i)))
    base_mask = (r_i // GDN_BASE) == (c_i // GDN_BASE)
    merge_masks = []
    size = GDN_BASE
    while size < CHUNK:
        merge_masks.append(((r_i // (2 * size)) == (c_i // (2 * size))) & ((r_i // size) != (c_i // size)))
        size *= 2
    scale = DK_A ** -0.5

    def recurrence(n, d, h):
        grp = n if d == 0 else n_groups - 1 - n
        r0 = grp * GDN_GROUP if isinstance(grp, int) else pl.multiple_of(grp * GDN_GROUP, GDN_GROUP)
        incl, strict = masks[d]
        jj = d * GDN_HB + h
        cols = slice(h * LANES, (h + 1) * LANES)
        gcol = gcol_ref[0, 0, grp][:, jj:jj + 1]
        beta = bcol_ref[0, 0, grp][:, jj:jj + 1]
        grow = grow_ref[0, 0, grp][jj:jj + 1, :]
        q_g = qn_ref[pl.ds(r0, GDN_GROUP), cols] * scale
        k_g = kn_ref[pl.ds(r0, GDN_GROUP), cols]
        v_g = vn_ref[pl.ds(r0, GDN_GROUP), cols]
        decay = jnp.where(incl, jnp.exp(jnp.where(incl, gcol - grow, 0.0)), 0.0)
        egc = jnp.exp(gcol)
        kb = k_g * beta
        kq = _bdot_nt(jnp.concatenate([kb, q_g], axis=0), k_g)
        yield
        lmat = jnp.where(strict, kq[:GDN_GROUP] * decay, 0.0)
        attn = jnp.where(incl, kq[GDN_GROUP:] * decay, 0.0)
        tmat = yield from _unit_tri_inverse(lmat, eye, base_mask, merge_masks)
        uw = _bdot(tmat, jnp.concatenate([v_g * beta, kb * egc], axis=1))
        yield
        u, w = uw[:, :DV_A], uw[:, DV_A:]
        qg = q_g * egc
        s_cur = s_ref[0, d, h]
        outs = [None] * GDN_GC
        for cc in (range(GDN_GC) if d == 0 else range(GDN_GC - 1, -1, -1)):
            rows = slice(cc * CHUNK, (cc + 1) * CHUNK)
            sidx = ((b * GDN_HG + hg) * n_chunks + grp * GDN_GC + cc) * GDN_NJ + jj
            ws_qs = _bdot(jnp.concatenate([w[rows], qg[rows]], axis=0), s_cur)
            yield
            v_new = u[rows] - ws_qs[:CHUNK]
            outs[cc] = ws_qs[CHUNK:] + _bdot(attn[rows, rows], v_new)
            k_dec = k_g[rows] * jnp.exp(gt_ref[sidx] - gcol[rows])
            s_cur = s_cur * egt_ref[sidx] + _bdot_tn(k_dec, v_new)
            yield
        return d, h, r0, cols, s_cur, jnp.concatenate(outs, axis=0)

    def group_step(n, carry):
        live = [recurrence(n, d, h) for d in range(2) for h in range(GDN_HB)]
        done = []
        while live:
            for gen in list(live):
                try:
                    next(gen)
                except StopIteration as stop:
                    live.remove(gen)
                    done.append(stop.value)
        for d, h, r0, cols, s_cur, o in done:
            s_ref[0, d, h] = s_cur
            (of_ref, ob_ref)[d][pl.ds(r0, GDN_GROUP), cols] = o
        return carry

    if n_groups == 1:
        group_step(0, 0)
    else:
        lax.fori_loop(0, n_groups, group_step, 0)

    for h in range(GDN_HB):
        cols = slice(h * LANES, (h + 1) * LANES)
        o = of_ref[:, cols] + ob_ref[:, cols]
        z = z_ref[0, :, cols]
        o = o * lax.rsqrt(jnp.mean(o * o, axis=-1, keepdims=True) + 1e-6) * nw_ref[...]
        o_ref[0, :, cols] = (o * (z * jax.nn.sigmoid(z))).astype(o_ref.dtype)


def _gdn_gates(ab, a_log, dt_bias):
    bn, t_len = ab.shape[:2]
    n = t_len // CHUNK
    a_a = ab[..., :2 * H_A].reshape(bn, t_len, 2, H_A)
    b_a = ab[..., 2 * H_A:].reshape(bn, t_len, 2, H_A)
    g = (-jnp.exp(a_log) * jax.nn.softplus(a_a + dt_bias)).reshape(bn, n, CHUNK, 2, H_A)
    beta = jax.nn.sigmoid(b_a).reshape(bn, n, CHUNK, 2, H_A)
    cs_f = jnp.cumsum(g[:, :, :, 0], axis=2)
    cs_b = jnp.flip(jnp.cumsum(jnp.flip(g[:, :, :, 1], axis=2), axis=2), axis=2)
    gcs = jnp.stack([cs_f, cs_b], axis=3)
    g_tot = jnp.stack([cs_f[:, :, -1], cs_b[:, :, 0]], axis=2)

    def group(t):
        t = t.reshape(bn, t_len // GDN_GROUP, GDN_GROUP, 2, GDN_HG, GDN_HB)
        return jnp.transpose(t, (0, 4, 1, 2, 3, 5)).reshape(bn, GDN_HG, t_len // GDN_GROUP, GDN_GROUP, GDN_NJ)

    gcol = group(gcs)
    grow = jnp.swapaxes(gcol, 3, 4)
    g_tot = jnp.transpose(g_tot.reshape(bn, n, 2, GDN_HG, GDN_HB), (0, 3, 1, 2, 4)).reshape(-1)
    return gcol, group(beta), grow, g_tot


def _gated_deltanet(proj, ab, conv_w, a_log, dt_bias, norm_w, s0):
    bn, t_len = proj.shape[:2]
    n = t_len // CHUNK
    gcol, bcol, grow, g_tot = _gdn_gates(ab, a_log, dt_bias)
    gw = GDN_HB * LANES
    off = lambda o: o // gw
    seq = lambda o: pl.BlockSpec((1, t_len, gw), lambda b, g, *_: (b, 0, off(o) + g))
    cw = lambda o: pl.BlockSpec((3, gw), lambda b, g, *_: (0, off(o) + g))
    ng = t_len // GDN_GROUP
    colspec = pl.BlockSpec((1, 1, ng, GDN_GROUP, GDN_NJ), lambda b, g, *_: (b, g, 0, 0, 0))
    state = pl.BlockSpec((1, 2, GDN_HB, DK_A, DV_A), lambda b, g, *_: (b, 0, g, 0, 0))
    in_specs = [seq(OFF_QKV_A), seq(OFF_QKV_A + A_W), seq(OFF_QKV_A + 2 * A_W), seq(OFF_Z_A),
                cw(0), cw(A_W), cw(2 * A_W), colspec, colspec,
                pl.BlockSpec((1, 1, ng, GDN_NJ, GDN_GROUP), lambda b, g, *_: (b, g, 0, 0, 0)),
                pl.BlockSpec((1, LANES), lambda b, g, *_: (0, 0))]
    args = [g_tot, jnp.exp(g_tot), proj, proj, proj, proj, conv_w, conv_w, conv_w, gcol, bcol, grow,
            norm_w.reshape(1, DV_A)]
    if s0 is not None:
        in_specs.append(state)
        args.append(s0)
    return pl.pallas_call(
        functools.partial(_gdn_kernel, t_len=t_len, has_s0=s0 is not None),
        grid_spec=pltpu.PrefetchScalarGridSpec(
            num_scalar_prefetch=2,
            grid=(bn, GDN_HG),
            in_specs=in_specs,
            out_specs=[pl.BlockSpec((1, t_len, gw), lambda b, g, *_: (b, 0, g)), state],
            scratch_shapes=[pltpu.VMEM((t_len, gw), F32)] * 5,
        ),
        out_shape=[jax.ShapeDtypeStruct((bn, t_len, A_W), BF16),
                   jax.ShapeDtypeStruct((bn, 2, H_A, DK_A, DV_A), F32)],
        compiler_params=_cparams(2),
        name="gated_deltanet",
    )(*args)


ATT_HB = 4


def _attn_kernel(lam_ref, q_ref, k_ref, v_ref, w_ref, o_ref, *, scale, post_scale):
    lam = lam_ref[0]
    heads = [slice(h * LANES, (h + 1) * LANES) for h in range(ATT_HB)]
    q = [q_ref[0, :, c].astype(BF16) for c in heads]
    k = [k_ref[0, :, c].astype(BF16) for c in heads]

    def scores(qm, km):
        return lax.dot_general(qm, km, (((1,), (1,)), ((), ())), preferred_element_type=F32) * scale

    def softmax(s):
        e = jnp.exp(s - jnp.max(s, axis=-1, keepdims=True))
        return e / jnp.sum(e, axis=-1, keepdims=True)

    s1 = [scores(q[h][:, :DH_B], k[h][:, :DH_B]) for h in range(ATT_HB)]
    s2 = [scores(q[h][:, DH_B:], k[h][:, DH_B:]) for h in range(ATT_HB)]
    a = [(softmax(s1[h]) - lam * softmax(s2[h])).astype(BF16) for h in range(ATT_HB)]
    o = [jnp.dot(a[h], v_ref[0, :, heads[h]].astype(BF16), preferred_element_type=F32) for h in range(ATT_HB)]
    for h in range(ATT_HB):
        oh = o[h] * lax.rsqrt(jnp.mean(o[h] * o[h], axis=-1, keepdims=True) + 1e-6)
        o_ref[0, :, heads[h]] = (oh * w_ref[...] * post_scale).astype(o_ref.dtype)


def _diff_attention(q, q_off, k, k_off, v, v_off, lam, subln_w, post_scale, bq=256):
    bn, tq = q.shape[:2]
    tk = k.shape[1]
    gw = ATT_HB * LANES
    assert q_off % ATT_HB == 0 and k_off % ATT_HB == 0 and v_off % ATT_HB == 0
    return pl.pallas_call(
        functools.partial(_attn_kernel, scale=DH_B ** -0.5, post_scale=post_scale),
        grid=(bn, H_B // ATT_HB, tq // bq),
        in_specs=[
            pl.BlockSpec(memory_space=pltpu.SMEM),
            pl.BlockSpec((1, bq, gw), lambda b, h, i: (b, i, q_off // ATT_HB + h)),
            pl.BlockSpec((1, tk, gw), lambda b, h, i: (b, 0, k_off // ATT_HB + h)),
            pl.BlockSpec((1, tk, gw), lambda b, h, i: (b, 0, v_off // ATT_HB + h)),
            pl.BlockSpec((1, LANES), lambda b, h, i: (0, 0)),
        ],
        out_specs=pl.BlockSpec((1, bq, gw), lambda b, h, i: (b, i, h)),
        out_shape=jax.ShapeDtypeStruct((bn, tq, B_W), BF16),
        compiler_params=_cparams(3),
        name="diff_attention",
    )(lam.reshape(1), q, k, v, subln_w.reshape(1, DV_B))


def _dft_mats(n):
    j = lax.broadcasted_iota(jnp.int32, (n, n), 0)
    k = lax.broadcasted_iota(jnp.int32, (n, n), 1)
    ang = ((j * k) % n).astype(F32) * (2.0 * math.pi / n)
    s = n ** -0.5
    return (jnp.cos(ang) * s).astype(BF16), (jnp.sin(ang) * s).astype(BF16)


def _dft_kernel(x_ref, cc_ref, sc_ref, ct_ref, st_ref, o_ref):
    x = x_ref[0].astype(BF16)
    for g in range(C_GROUPS):
        xg = x[:, g * C_GW:(g + 1) * C_GW]
        a = jnp.dot(xg, cc_ref[...], preferred_element_type=F32).astype(BF16)
        b = jnp.dot(xg, sc_ref[...], preferred_element_type=F32).astype(BF16)
        o = jnp.dot(ct_ref[...], a, preferred_element_type=F32) - jnp.dot(st_ref[...], b, preferred_element_type=F32)
        o_ref[0, :, g * C_GW:(g + 1) * C_GW] = o.astype(o_ref.dtype)


def _fourier_mix(x, x_off):
    bn, t = x.shape[:2]
    cc, sc = _dft_mats(C_GW)
    ct, st = _dft_mats(t)
    full = lambda shape: pl.BlockSpec(shape, lambda b: (0, 0))
    return pl.pallas_call(
        _dft_kernel,
        grid=(bn,),
        in_specs=[
            pl.BlockSpec((1, t, C_W), lambda b: (b, 0, x_off)),
            full((C_GW, C_GW)), full((C_GW, C_GW)), full((t, t)), full((t, t)),
        ],
        out_specs=pl.BlockSpec((1, t, C_W), lambda b: (b, 0, 0)),
        out_shape=jax.ShapeDtypeStruct((bn, t, C_W), BF16),
        compiler_params=_cparams(1),
        name="fourier_mix",
    )(x, cc, sc, ct, st)


MOE_BM = 512
MOE_TF = 512
MOE_NF = D_FF // MOE_TF
MOE_ROWS = -(-(N_TOK * TOP_K + N_EXPERTS * (MOE_BM - 1)) // MOE_BM) * MOE_BM
MOE_NB = MOE_ROWS // MOE_BM


def _moe_kernel(be_ref, nv_ref, x_ref, wgu_ref, bgu_ref, wd_ref, bd_ref, o_ref, wdp_ref):
    i = pl.program_id(0)
    j = pl.program_id(1)
    half = MOE_TF // 2
    n_chunk = MOE_TF // LANES

    @pl.when(i < nv_ref[0])
    def _():
        hb = jnp.dot(x_ref[...].astype(BF16), wgu_ref[...].astype(BF16), preferred_element_type=F32) + bgu_ref[...]
        even = (lax.broadcasted_iota(jnp.int32, (MOE_BM, LANES), 1) % 2) == 0
        acts = []
        for c in range(n_chunk):
            lo = hb[:, c * LANES:(c + 1) * LANES]
            hi = hb[:, (c + n_chunk) * LANES:(c + n_chunk + 1) * LANES]
            gate = jnp.where(even, lo, pltpu.roll(hi, 1, 1))
            up = jnp.where(even, pltpu.roll(lo, LANES - 1, 1), hi)
            gate = jnp.minimum(gate, SWIGLU_LIMIT)
            up = jnp.clip(up, -SWIGLU_LIMIT, SWIGLU_LIMIT)
            acts.append(((up + 1.0) * gate * jax.nn.sigmoid(SWIGLU_ALPHA * gate)).astype(BF16))
        act = jnp.concatenate(acts, axis=1)
        for cb in range(D_MODEL // LANES):
            cols = slice(cb * LANES, (cb + 1) * LANES)
            wdp_ref[cb, pl.ds(0, half, stride=2), :] = wd_ref[0:half, cols]
            wdp_ref[cb, pl.ds(1, half, stride=2), :] = wd_ref[half:MOE_TF, cols]
        wdp = jnp.concatenate([wdp_ref[cb].astype(BF16) for cb in range(D_MODEL // LANES)], axis=1)
        contrib = jnp.dot(act, wdp, preferred_element_type=F32)

        @pl.when(j == 0)
        def _():
            o_ref[...] = contrib + bd_ref[...]

        @pl.when(j > 0)
        def _():
            o_ref[...] += contrib

    @pl.when((i >= nv_ref[0]) & (j == 0))
    def _():
        o_ref[...] = jnp.zeros_like(o_ref)


def _moe_ffn(x_disp, block_e, n_valid, w_gu, b_gu, w_down, b_down, l):
    def blk(i, nv):
        return jnp.minimum(i, nv[0] - 1)

    def ftile(i, j, nv):
        return jnp.where(i < nv[0], j, MOE_NF - 1)

    return pl.pallas_call(
        _moe_kernel,
        grid_spec=pltpu.PrefetchScalarGridSpec(
            num_scalar_prefetch=2,
            grid=(MOE_NB, MOE_NF),
            in_specs=[
                pl.BlockSpec((MOE_BM, D_MODEL), lambda i, j, be, nv: (blk(i, nv), 0)),
                pl.BlockSpec((None, None, D_MODEL, 2 * MOE_TF), lambda i, j, be, nv: (l, be[i], 0, ftile(i, j, nv))),
                pl.BlockSpec((None, 1, 2 * MOE_TF), lambda i, j, be, nv: (l * N_EXPERTS + be[i], 0, ftile(i, j, nv))),
                pl.BlockSpec((None, None, MOE_TF, D_MODEL), lambda i, j, be, nv: (l, be[i], ftile(i, j, nv), 0)),
                pl.BlockSpec((None, 1, D_MODEL), lambda i, j, be, nv: (l * N_EXPERTS + be[i], 0, 0)),
            ],
            out_specs=pl.BlockSpec((MOE_BM, D_MODEL), lambda i, j, be, nv: (i, 0)),
            scratch_shapes=[pltpu.VMEM((D_MODEL // LANES, MOE_TF, LANES), F32)],
        ),
        out_shape=jax.ShapeDtypeStruct((MOE_ROWS, D_MODEL), F32),
        compiler_params=_cparams(2),
        name="moe_ffn",
    )(block_e, n_valid, x_disp, w_gu, b_gu.reshape(DEPTH * N_EXPERTS, 1, 2 * D_FF), w_down,
      b_down.reshape(DEPTH * N_EXPERTS, 1, D_MODEL))


CMB_TB = 128
CMB_STEPS = N_TOK // CMB_TB
CMB_ROWS = CMB_TB * TOP_K


def _combine_kernel(dest_ref, y_hbm, gate_ref, x_ref, g2_ref, lng_ref, lnb_ref, o_ref, buf_ref, sem_ref):
    i = pl.program_id(0)
    slot = i % 2

    def row_copy(src_row, s, k, t):
        return pltpu.make_async_copy(y_hbm.at[pl.ds(src_row, 1), :], buf_ref.at[s, k, pl.ds(t, 1), :], sem_ref.at[s])

    def issue(block, s):
        def body(t, carry):
            for k in range(TOP_K):
                row_copy(dest_ref[(block * CMB_TB + t) * TOP_K + k], s, k, t).start()
            return carry
        lax.fori_loop(0, CMB_TB, body, 0)

    @pl.when(i == 0)
    def _():
        issue(0, 0)

    @pl.when(i + 1 < CMB_STEPS)
    def _():
        issue(i + 1, 1 - slot)

    def wait_body(t, carry):
        for k in range(TOP_K):
            row_copy(0, slot, k, t).wait()
        return carry
    lax.fori_loop(0, CMB_TB, wait_body, 0)

    gate = gate_ref[...]
    y = None
    for k in range(TOP_K):
        term = buf_ref[slot, k] * gate[:, k:k + 1]
        y = term if y is None else y + term
    o_ref[...] = _ln_rows(DN_ALPHA * x_ref[...] + g2_ref[...] * y) * lng_ref[...] + lnb_ref[...]


def _moe_combine(y_disp, dest, gate, x, mod3, ln_g, ln_b):
    return pl.pallas_call(
        _combine_kernel,
        grid_spec=pltpu.PrefetchScalarGridSpec(
            num_scalar_prefetch=1,
            grid=(CMB_STEPS,),
            in_specs=[
                pl.BlockSpec(memory_space=pl.ANY),
                pl.BlockSpec((CMB_TB, TOP_K), lambda i, d: (i, 0)),
                pl.BlockSpec((CMB_TB, D_MODEL), lambda i, d: (i, 0)),
                pl.BlockSpec((None, 1, D_MODEL), lambda i, d: (_cond_block(i * CMB_TB), 0, 5)),
                pl.BlockSpec((1, D_MODEL), lambda i, d: (0, 0)),
                pl.BlockSpec((1, D_MODEL), lambda i, d: (0, 0)),
            ],
            out_specs=pl.BlockSpec((CMB_TB, D_MODEL), lambda i, d: (i, 0)),
            scratch_shapes=[pltpu.VMEM((2, TOP_K, CMB_TB, D_MODEL), F32), pltpu.SemaphoreType.DMA((2,))],
        ),
        out_shape=jax.ShapeDtypeStruct((N_TOK, D_MODEL), F32),
        compiler_params=_cparams(1),
        name="moe_combine",
    )(dest, y_disp, gate, x, mod3, ln_g.reshape(1, D_MODEL), ln_b.reshape(1, D_MODEL))


def _moe_route(h, router_w, router_b):
    n_assign = N_TOK * TOP_K
    rw = jnp.pad(router_w, ((0, 0), (0, LANES - N_EXPERTS)))
    rb = jnp.pad(router_b, (0, LANES - N_EXPERTS)).reshape(1, LANES)
    logits = _router_logits(h, rw, rb)[:, :N_EXPERTS]
    top_v, top_i = lax.top_k(logits, TOP_K)
    gate = jax.nn.softmax(top_v, axis=-1)
    e_flat = top_i.reshape(-1)
    order = jnp.argsort(e_flat)
    e_sorted = e_flat[order]
    tok_sorted = (order // TOP_K).astype(jnp.int32)
    counts = jnp.bincount(e_flat, length=N_EXPERTS)
    padded = (counts + MOE_BM - 1) // MOE_BM * MOE_BM
    pad_end = jnp.cumsum(padded)
    pad_start = pad_end - padded
    start = jnp.cumsum(counts) - counts
    dest_sorted = (pad_start[e_sorted] + (jnp.arange(n_assign) - start[e_sorted])).astype(jnp.int32)
    dest = jnp.zeros((n_assign,), jnp.int32).at[order].set(dest_sorted)
    src_tok = jnp.zeros((MOE_ROWS,), jnp.int32).at[dest_sorted].set(tok_sorted)
    n_valid = (pad_end[-1] // MOE_BM).astype(jnp.int32)
    blocks = jnp.arange(MOE_NB, dtype=jnp.int32)
    block_e = jnp.searchsorted(pad_end, jnp.minimum(blocks, n_valid - 1) * MOE_BM, side="right")
    block_e = jnp.minimum(block_e, N_EXPERTS - 1).astype(jnp.int32)
    return gate, dest, src_tok, block_e, n_valid.reshape(1)


def _silu(x):
    return x * jax.nn.sigmoid(x)


def _rope_tables():
    rows = DEC_SEQ // GRID_W
    row = jnp.repeat(jnp.arange(rows, dtype=F32), GRID_W)
    col = jnp.tile(jnp.arange(GRID_W, dtype=F32), rows)
    inv_freq = ROPE_BASE ** (-jnp.arange(ROPE_F, dtype=F32) / ROPE_F)
    ang = jnp.stack([row[:, None] * inv_freq, col[:, None] * inv_freq], axis=1)
    return jnp.cos(ang), jnp.sin(ang)


def _rope_2d(x, cos, sin):
    xs = x.reshape(x.shape[:-1] + (2, 2, ROPE_F))
    a, b = xs[..., 0, :], xs[..., 1, :]
    cs, sn = cos[:, None, None], sin[:, None, None]
    out = jnp.stack([a * cs - b * sn, b * cs + a * sn], axis=-2)
    return out.reshape(x.shape)


def _lam(lam_params, l):
    lam_init = 0.8 - 0.6 * math.exp(-0.3 * l)
    lam = jnp.exp(jnp.sum(lam_params[0] * lam_params[1])) - jnp.exp(jnp.sum(lam_params[2] * lam_params[3])) + lam_init
    return lam, lam_init


def kernel(x_prompt, x_sample, cache_k, cache_v, state_delta, c, c_ctx, w_ada, b_ada, w_in, conv_w, a_log, dt_bias, gdn_norm_w, lam_params, subln_w, w_br, w_gate, b_gate, w_out, ln1_g, ln1_b, ln2_g, ln2_b, router_w, router_b, w_gu, b_gu, w_down, b_down):
    d = D_MODEL
    x = jnp.concatenate([x_prompt.reshape(N_CTX, d), x_sample.reshape(N_LAT, d)], axis=0)

    cond = jnp.concatenate([_silu(c_ctx)[None, :], _silu(c)], axis=0)
    cond = jnp.pad(cond, ((0, N_COND - cond.shape[0]), (0, 0))).astype(BF16)
    rope_cos, rope_sin = _rope_tables()
    w_gate2 = w_gate.reshape(DEPTH, d, N_BRANCH * d)
    w_br2 = w_br.reshape(DEPTH * N_BRANCH, BR_W, d)

    ks, vs, ss = [], [], []
    for l in range(DEPTH):
        mod3 = _mm(cond, w_ada, l, bias=b_ada[l][None, :], bm=N_COND).reshape(N_COND, 1, 6 * d)
        h = _premix(x, mod3)
        w_l = w_in[l]
        w_main = jnp.concatenate([w_l[:, :OFF_Q_B], w_l[:, OFF_Q_B + 4 * H_A:]], axis=1).astype(BF16)
        w_ab = jnp.pad(w_l[:, OFF_Q_B:OFF_Q_B + 4 * H_A], ((0, 0), (0, LANES - 4 * H_A))).astype(BF16)
        proj = _mm(h, w_main[None], 0)
        ab = _mm(h, w_ab[None], 0)[:, :4 * H_A]

        proj_c = proj[:N_CTX].reshape(BATCH, SEQ, MAIN_W)
        proj_s = proj[N_CTX:].reshape(DEC_BATCH, DEC_SEQ, MAIN_W)
        ab_c = ab[:N_CTX].reshape(BATCH, SEQ, 4 * H_A)
        ab_s = ab[N_CTX:].reshape(DEC_BATCH, DEC_SEQ, 4 * H_A)

        oa_c, s_new = _gated_deltanet(proj_c, ab_c, conv_w[l], a_log[l], dt_bias[l], gdn_norm_w[l], None)
        oa_s, _ = _gated_deltanet(proj_s, ab_s, conv_w[l], a_log[l], dt_bias[l], gdn_norm_w[l], state_delta[:, l])

        lam, lam_init = _lam(lam_params[l], l)
        qo, ko, vo = OFF_Q_B // LANES, OFF_K_B // LANES, OFF_V_B // LANES
        ob_c = _diff_attention(proj_c, qo, proj_c, ko, proj_c, vo, lam, subln_w[l], 1.0 - lam_init)
        q_s = proj_s[..., OFF_Q_B:OFF_K_B].reshape(DEC_BATCH, DEC_SEQ, H_B, 2, DH_B)
        k_s = proj_s[..., OFF_K_B:OFF_V_B].reshape(DEC_BATCH, DEC_SEQ, H_B, 2, DH_B)
        q_r = _rope_2d(q_s, rope_cos, rope_sin).reshape(DEC_BATCH, DEC_SEQ, B_W).astype(BF16)
        keys = jnp.concatenate([_rope_2d(k_s, rope_cos, rope_sin).reshape(DEC_BATCH, DEC_SEQ, B_W),
                                cache_k[:, l].reshape(DEC_BATCH, PAST_LEN, B_W)], axis=1).astype(BF16)
        vals = jnp.concatenate([proj_s[..., OFF_V_B:OFF_X_C],
                                cache_v[:, l].reshape(DEC_BATCH, PAST_LEN, B_W)], axis=1).astype(BF16)
        ob_s = _diff_attention(q_r, 0, keys, 0, vals, 0, lam, subln_w[l], 1.0 - lam_init)

        oc_c = _fourier_mix(proj_c, OFF_X_C // C_W)
        oc_s = _fourier_mix(proj_s, OFF_X_C // C_W)

        ks.append(proj_c[..., OFF_K_B:OFF_V_B].reshape(BATCH, SEQ, H_B, 2, DH_B))
        vs.append(proj_c[..., OFF_V_B:OFF_X_C].reshape(BATCH, SEQ, H_B, DV_B))
        ss.append(s_new)

        both = lambda a_c, a_s: jnp.concatenate([a_c.reshape(N_CTX, BR_W), a_s.reshape(N_LAT, BR_W)], axis=0)
        merged = _gated_merge(h, (both(oa_c, oa_s), both(ob_c, ob_s), both(oc_c, oc_s)), w_gate2, b_gate, w_br2, l)
        mix = _mm(merged, w_out, l)

        x, h2 = _postmix(x, mix, mod3, ln1_g[l], ln1_b[l])

        gate, dest, src_tok, block_e, n_valid = _moe_route(h2, router_w[l], router_b[l])
        y_disp = _moe_ffn(h2[src_tok], block_e, n_valid, w_gu, b_gu, w_down, b_down, l)
        x = _moe_combine(y_disp, dest, gate, x, mod3, ln2_g[l], ln2_b[l])

    y_prompt = x[:N_CTX].reshape(BATCH, SEQ, d)
    y_sample = x[N_CTX:].reshape(DEC_BATCH, DEC_SEQ, d)
    return (y_prompt, y_sample, jnp.stack(ks, axis=1), jnp.stack(vs, axis=1), jnp.stack(ss, axis=1))
```
